```python
import jax, jax.numpy as jnp
from jax import lax
import numpy as np

D_MODEL = 4096
BATCH = 4
SEQ = 2048
DEPTH = 1

D_MIX = D_MODEL
D_CONV = D_MIX // 2
D_SGU = D_MIX - D_CONV
CONV_GROUPS = 16
SGU_HEADS = 16
SGU_HEAD_DIM = D_SGU // SGU_HEADS
CHUNK = 128
CONV_WIDTH = 31
D_FF = 11008
FFN_CONV_WIDTH = 3
N_MOD = 6
EPS = 1e-6

kernel_name = "hybrid_conformer_conv_sgu_adaln_block"


def rms_norm(x, g):
    xf = x.astype(jnp.float32)
    y = xf * lax.rsqrt(jnp.mean(xf * xf, axis=-1, keepdims=True) + EPS)
    return (y * g.astype(jnp.float32)).astype(x.dtype)


def layer_norm(x, g, b):
    xf = x.astype(jnp.float32)
    mu = jnp.mean(xf, axis=-1, keepdims=True)
    xc = xf - mu
    var = jnp.mean(xc * xc, axis=-1, keepdims=True)
    y = xc * lax.rsqrt(var + EPS) * g.astype(jnp.float32) + b.astype(jnp.float32)
    return y.astype(x.dtype)


def depthwise_conv_centred(x, w, b):
    k = w.shape[0]
    pad = (k - 1) // 2
    y = lax.conv_general_dilated(
        x, w[:, None, :].astype(x.dtype), window_strides=(1,), padding=[(pad, pad)],
        dimension_numbers=("NWC", "WIO", "NWC"), feature_group_count=x.shape[-1])
    return y + b


def modulate(x, g, shift, scale):
    return rms_norm(x, g) * (1 + scale[:, None, :]) + shift[:, None, :]


def setup_inputs(seed: int = 0) -> dict:
    key = jax.random.key(seed)
    ks = jax.random.split(key, 24)
    L = DEPTH

    def nrm(k, shape, std):
        return std * jax.random.normal(k, shape, jnp.float32)

    return {
        "x": nrm(ks[0], (BATCH, SEQ, D_MODEL), 1.0),
        "c": nrm(ks[1], (BATCH, D_MODEL), 1.0),
        "w_ada": nrm(ks[2], (L, D_MODEL, N_MOD * D_MODEL), 0.5 * D_MODEL ** -0.5),
        "b_ada": nrm(ks[3], (L, N_MOD * D_MODEL), 0.02),
        "g_mix": 1.0 + nrm(ks[4], (L, D_MODEL), 0.02),
        "w_in": nrm(ks[5], (L, D_MODEL, 2 * D_MIX), D_MODEL ** -0.5),
        "conv_w": nrm(ks[6], (L, CONV_WIDTH, D_CONV), CONV_WIDTH ** -0.5),
        "conv_b": nrm(ks[7], (L, D_CONV), 0.02),
        "conv_ln_g": 1.0 + nrm(ks[8], (L, D_CONV), 0.02),
        "conv_ln_b": nrm(ks[9], (L, D_CONV), 0.02),
        "sgu_ln_g": 1.0 + nrm(ks[10], (L, D_SGU), 0.02),
        "sgu_ln_b": nrm(ks[11], (L, D_SGU), 0.02),
        "sgu_w": nrm(ks[12], (L, SGU_HEADS, CHUNK, CHUNK), CHUNK ** -0.5),
        "sgu_b": 1.0 + nrm(ks[13], (L, SGU_HEADS, CHUNK), 0.02),
        "out_g_conv": 1.0 + nrm(ks[14], (L, D_CONV), 0.02),
        "out_g_sgu": 1.0 + nrm(ks[15], (L, D_SGU), 0.02),
        "w_out": nrm(ks[16], (L, D_MIX, D_MODEL), D_MIX ** -0.5),
        "g_ffn": 1.0 + nrm(ks[17], (L, D_MODEL), 0.02),
        "w_up": nrm(ks[18], (L, D_MODEL, 2 * D_FF), D_MODEL ** -0.5),
        "ffn_conv_w": nrm(ks[19], (L, FFN_CONV_WIDTH, D_FF), FFN_CONV_WIDTH ** -0.5),
        "ffn_conv_b": nrm(ks[20], (L, D_FF), 0.02),
        "w_down": nrm(ks[21], (L, D_FF, D_MODEL), D_FF ** -0.5),
        "g_final": 1.0 + nrm(ks[22], (D_MODEL,), 0.02),
    }


def reference(x, c, w_ada, b_ada, g_mix, w_in, conv_w, conv_b, conv_ln_g, conv_ln_b,
              sgu_ln_g, sgu_ln_b, sgu_w, sgu_b, out_g_conv, out_g_sgu, w_out,
              g_ffn, w_up, ffn_conv_w, ffn_conv_b, w_down, g_final):
    bsz, seq, _ = x.shape
    n_chunks = seq // CHUNK
    c_act = jax.nn.silu(c)

    for l in range(DEPTH):
        mod = (c_act @ w_ada[l] + b_ada[l]).reshape(bsz, N_MOD, D_MODEL)
        shift_m, scale_m, gate_m = mod[:, 0], mod[:, 1], mod[:, 2]
        shift_f, scale_f, gate_f = mod[:, 3], mod[:, 4], mod[:, 5]

        h = modulate(x, g_mix[l], shift_m, scale_m)
        proj = h @ w_in[l]
        p_conv = proj[..., :2 * D_CONV]
        p_sgu = proj[..., 2 * D_CONV:]

        a = p_conv[..., :D_CONV] * jax.nn.sigmoid(p_conv[..., D_CONV:])
        a = depthwise_conv_centred(a, conv_w[l], conv_b[l])
        a = jax.nn.silu(layer_norm(a, conv_ln_g[l], conv_ln_b[l]))

        z = jax.nn.gelu(p_sgu, approximate=False)
        u, v = z[..., :D_SGU], z[..., D_SGU:]
        v = layer_norm(v, sgu_ln_g[l], sgu_ln_b[l])
        v = v.reshape(bsz, n_chunks, CHUNK, SGU_HEADS, SGU_HEAD_DIM)
        v = jnp.einsum("hpq,bcqhd->bcphd", sgu_w[l], v) + sgu_b[l].T[:, :, None]
        bgrp = u * v.reshape(bsz, seq, D_SGU)

        y = jnp.concatenate([rms_norm(a, out_g_conv[l]), rms_norm(bgrp, out_g_sgu[l])], axis=-1)
        x = x + gate_m[:, None, :] * (y @ w_out[l])

        h = modulate(x, g_ffn[l], shift_f, scale_f)
        up = h @ w_up[l]
        gte = depthwise_conv_centred(up[..., :D_FF], ffn_conv_w[l], ffn_conv_b[l])
        act = jax.nn.silu(gte) * up[..., D_FF:]
        x = x + gate_f[:, None, :] * (act @ w_down[l])

    return rms_norm(x, g_final)
```

```python
import functools
import math

import jax
import jax.numpy as jnp
from jax import lax
from jax.experimental import pallas as pl
from jax.experimental.pallas import tpu as pltpu

EPS = 1e-6
CHUNK = 128
HEAD_DIM = 128
HALO = 16
N_MOD_PAD = 8
VMEM_LIMIT_BYTES = 58 * 1024 * 1024

F32 = jnp.float32
BF16 = jnp.bfloat16


def _dot(a, b):
    return jnp.dot(a, b, preferred_element_type=F32)


def _silu(x):
    return x * jax.nn.sigmoid(x)


def _rms_rows(x, g):
    return x * lax.rsqrt(jnp.mean(x * x, axis=-1, keepdims=True) + EPS) * g


def _modulate(x, g, scale, shift):
    return _rms_rows(x, g) * (1.0 + scale) + shift


def _params(sem):
    return pltpu.CompilerParams(dimension_semantics=sem, vmem_limit_bytes=VMEM_LIMIT_BYTES)


def _ada_kernel(c_ref, w_ref, b_ref, o_ref):
    ca = _silu(c_ref[...]).astype(BF16)
    o_ref[...] = _dot(ca, w_ref[...].astype(BF16)) + b_ref[...]


def _ada(c_pad, w_ada, b_ada, tn=512):
    rows, d = c_pad.shape
    n = w_ada.shape[1]
    return pl.pallas_call(
        _ada_kernel,
        grid=(n // tn,),
        in_specs=[
            pl.BlockSpec((rows, d), lambda j: (0, 0)),
            pl.BlockSpec((d, tn), lambda j: (0, j)),
            pl.BlockSpec((1, tn), lambda j: (0, j)),
        ],
        out_specs=pl.BlockSpec((rows, tn), lambda j: (0, j)),
        out_shape=jax.ShapeDtypeStruct((rows, n), F32),
        compiler_params=_params(("arbitrary",)),
        name="ada",
    )(c_pad, w_ada, b_ada)


def _inproj_kernel(x_ref, mod_ref, g_ref, wm_ref, wg_ref, a_ref, z_ref, h_scr, *, n_glu, rows):
    j = pl.program_id(1)

    @pl.when(j == 0)
    def _():
        g = g_ref[...]
        shift = mod_ref[0, 0:1, :]
        scale = mod_ref[0, 1:2, :]

        def body(r, carry):
            sl = pl.ds(pl.multiple_of(r * rows, rows), rows)
            h_scr[sl, :] = _modulate(x_ref[sl, :], g, scale, shift).astype(BF16)
            return carry

        lax.fori_loop(0, x_ref.shape[0] // rows, body, 0)

    @pl.when(j < n_glu)
    def _():
        h = h_scr[...]
        a_ref[...] = _dot(h, wm_ref[...]) * jax.nn.sigmoid(_dot(h, wg_ref[...]))

    @pl.when(j >= n_glu)
    def _():
        p = _dot(h_scr[...], wm_ref[...])
        z_ref[...] = 0.5 * p * (1.0 + lax.erf(p * math.sqrt(0.5)))


def _inproj(x2d, mod, g_mix, w_in, d_conv, seq, tm=512, tn=512):
    t, d = x2d.shape
    n_glu = d_conv // tn
    n_z = (w_in.shape[1] - 2 * d_conv) // tn
    tiles_per_seq = seq // tm
    kern = functools.partial(_inproj_kernel, n_glu=n_glu, rows=64)
    return pl.pallas_call(
        kern,
        grid=(t // tm, n_glu + n_z),
        in_specs=[
            pl.BlockSpec((tm, d), lambda i, j: (i, 0)),
            pl.BlockSpec((1, N_MOD_PAD, d), lambda i, j: (i // tiles_per_seq, 0, 0)),
            pl.BlockSpec((1, d), lambda i, j: (0, 0)),
            pl.BlockSpec((d, tn), lambda i, j: (0, j + jnp.where(j >= n_glu, n_glu, 0))),
            pl.BlockSpec((d, tn), lambda i, j: (0, jnp.minimum(j + n_glu, 2 * n_glu - 1))),
        ],
        out_specs=[
            pl.BlockSpec((tm, tn), lambda i, j: (i, jnp.minimum(j, n_glu - 1))),
            pl.BlockSpec((tm, tn), lambda i, j: (i, jnp.maximum(j - n_glu, 0))),
        ],
        out_shape=[
            jax.ShapeDtypeStruct((t, d_conv), F32),
            jax.ShapeDtypeStruct((t, n_z * tn), F32),
        ],
        scratch_shapes=[pltpu.VMEM((tm, d), BF16)],
        compiler_params=_params(("arbitrary", "arbitrary")),
        name="inproj",
    )(x2d, mod, g_mix, w_in, w_in)


def _convbr_kernel(prev_ref, main_ref, next_ref, w_ref, cb_ref, lg_ref, lb_ref, og_ref, y_ref,
                   buf, cbuf, *, ts, tiles_per_seq, taps, lane_chunk, rows):
    i = pl.program_id(0)
    first = (i % tiles_per_seq) == 0
    last = (i % tiles_per_seq) == tiles_per_seq - 1
    buf[0:HALO, :] = jnp.where(first, 0.0, prev_ref[...])
    buf[HALO:HALO + ts, :] = main_ref[...]
    buf[HALO + ts:2 * HALO + ts, :] = jnp.where(last, 0.0, next_ref[...])
    c = main_ref.shape[1]
    off = HALO - (taps - 1) // 2

    for c0 in range(0, c, lane_chunk):
        cs = slice(c0, c0 + lane_chunk)
        cb = cb_ref[:, cs]

        def conv_body(r, carry):
            base = pl.multiple_of(r * rows, rows)
            win = buf[pl.ds(base, rows + 2 * HALO), cs]
            acc = jnp.zeros((rows, lane_chunk), F32) + cb
            for k in range(taps):
                acc = acc + w_ref[k:k + 1, cs] * win[off + k:off + k + rows, :]
            cbuf[pl.ds(base, rows), cs] = acc
            return carry

        lax.fori_loop(0, ts // rows, conv_body, 0)

    lg = lg_ref[...]
    lb = lb_ref[...]
    og = og_ref[...]

    def norm_body(r, carry):
        sl = pl.ds(pl.multiple_of(r * rows, rows), rows)
        a = cbuf[sl, :]
        xc = a - jnp.mean(a, axis=-1, keepdims=True)
        var = jnp.mean(xc * xc, axis=-1, keepdims=True)
        s = _silu(xc * lax.rsqrt(var + EPS) * lg + lb)
        y_ref[sl, :] = _rms_rows(s, og).astype(BF16)
        return carry

    lax.fori_loop(0, ts // rows, norm_body, 0)


def _convbr(a_glu, conv_w, conv_b, ln_g, ln_b, out_g, seq, ts=256):
    t, c = a_glu.shape
    taps = conv_w.shape[0]
    w_pad = jnp.pad(conv_w, ((0, (-taps) % 8), (0, 0)))
    tiles_per_seq = seq // ts
    hb = ts // HALO
    n_hb = t // HALO
    kern = functools.partial(_convbr_kernel, ts=ts, tiles_per_seq=tiles_per_seq, taps=taps,
                             lane_chunk=512, rows=16)
    row = lambda i: (0, 0)
    return pl.pallas_call(
        kern,
        grid=(t // ts,),
        in_specs=[
            pl.BlockSpec((HALO, c), lambda i: (jnp.maximum(i * hb - 1, 0), 0)),
            pl.BlockSpec((ts, c), lambda i: (i, 0)),
            pl.BlockSpec((HALO, c), lambda i: (jnp.minimum((i + 1) * hb, n_hb - 1), 0)),
            pl.BlockSpec(w_pad.shape, row),
            pl.BlockSpec((1, c), row),
            pl.BlockSpec((1, c), row),
            pl.BlockSpec((1, c), row),
            pl.BlockSpec((1, c), row),
        ],
        out_specs=pl.BlockSpec((ts, c), lambda i: (i, 0)),
        out_shape=jax.ShapeDtypeStruct((t, c), BF16),
        scratch_shapes=[pltpu.VMEM((ts + 2 * HALO, c), F32), pltpu.VMEM((ts, c), F32)],
        compiler_params=_params(("arbitrary",)),
        name="convbr",
    )(a_glu, a_glu, a_glu, w_pad, conv_b, ln_g, ln_b, out_g)


def _sgubr_kernel(u_ref, v_ref, w_ref, bias_ref, lg_ref, lb_ref, og_ref, y_ref, vn_scr, b_scr,
                  *, ts, rows):
    lg = lg_ref[...]
    lb = lb_ref[...]
    og = og_ref[...]
    n_chunks = ts // CHUNK
    n_heads = v_ref.shape[1] // HEAD_DIM

    def ln_body(r, carry):
        sl = pl.ds(pl.multiple_of(r * rows, rows), rows)
        v = v_ref[sl, :]
        xc = v - jnp.mean(v, axis=-1, keepdims=True)
        var = jnp.mean(xc * xc, axis=-1, keepdims=True)
        vn_scr[sl, :] = (xc * lax.rsqrt(var + EPS) * lg + lb).astype(BF16)
        return carry

    lax.fori_loop(0, ts // rows, ln_body, 0)

    for h in range(n_heads):
        hs = slice(h * HEAD_DIM, (h + 1) * HEAD_DIM)
        rhs = jnp.concatenate(
            [vn_scr[ck * CHUNK:(ck + 1) * CHUNK, hs] for ck in range(n_chunks)], axis=1)
        mixed = _dot(w_ref[h], rhs)
        bias = bias_ref[:, hs]
        for ck in range(n_chunks):
            rs = slice(ck * CHUNK, (ck + 1) * CHUNK)
            m = mixed[:, ck * HEAD_DIM:(ck + 1) * HEAD_DIM] + bias
            b_scr[rs, hs] = u_ref[rs, hs] * m

    def rms_body(r, carry):
        sl = pl.ds(pl.multiple_of(r * rows, rows), rows)
        y_ref[sl, :] = _rms_rows(b_scr[sl, :], og).astype(BF16)
        return carry

    lax.fori_loop(0, ts // rows, rms_body, 0)


def _sgubr(z, sgu_w, bias_full, ln_g, ln_b, out_g, ts=256):
    t, two_c = z.shape
    c = two_c // 2
    kern = functools.partial(_sgubr_kernel, ts=ts, rows=16)
    row = lambda i: (0, 0)
    return pl.pallas_call(
        kern,
        grid=(t // ts,),
        in_specs=[
            pl.BlockSpec((ts, c), lambda i: (i, 0)),
            pl.BlockSpec((ts, c), lambda i: (i, 1)),
            pl.BlockSpec(sgu_w.shape, lambda i: (0, 0, 0)),
            pl.BlockSpec(bias_full.shape, row),
            pl.BlockSpec((1, c), row),
            pl.BlockSpec((1, c), row),
            pl.BlockSpec((1, c), row),
        ],
        out_specs=pl.BlockSpec((ts, c), lambda i: (i, 0)),
        out_shape=jax.ShapeDtypeStruct((t, c), BF16),
        scratch_shapes=[pltpu.VMEM((ts, c), BF16), pltpu.VMEM((ts, c), F32)],
        compiler_params=_params(("arbitrary",)),
        name="sgubr",
    )(z, z, sgu_w, bias_full, ln_g, ln_b, out_g)


def _outproj_kernel(ya_ref, yb_ref, wa_ref, wb_ref, x_ref, mod_ref, o_ref):
    y = _dot(ya_ref[...], wa_ref[...]) + _dot(yb_ref[...], wb_ref[...])
    o_ref[...] = x_ref[...] + mod_ref[0, 2:3, :] * y


def _outproj(ya, yb, w_out, x2d, mod, seq, tm=512, tn=512):
    t, c = ya.shape
    d = x2d.shape[1]
    tiles_per_seq = seq // tm
    return pl.pallas_call(
        _outproj_kernel,
        grid=(t // tm, d // tn),
        in_specs=[
            pl.BlockSpec((tm, c), lambda i, j: (i, 0)),
            pl.BlockSpec((tm, c), lambda i, j: (i, 0)),
            pl.BlockSpec((c, tn), lambda i, j: (0, j)),
            pl.BlockSpec((c, tn), lambda i, j: (1, j)),
            pl.BlockSpec((tm, tn), lambda i, j: (i, j)),
            pl.BlockSpec((1, N_MOD_PAD, tn), lambda i, j: (i // tiles_per_seq, 0, j)),
        ],
        out_specs=pl.BlockSpec((tm, tn), lambda i, j: (i, j)),
        out_shape=jax.ShapeDtypeStruct((t, d), F32),
        compiler_params=_params(("arbitrary", "arbitrary")),
        name="outproj",
    )(ya, yb, w_out, w_out, x2d, mod)


def _ffn_kernel(prev_ref, main_ref, next_ref, mod_ref, g_ref, wg_ref, wv_ref, cw_ref, cb_ref,
                wd_ref, gf_ref, o_ref, h_scr, g_scr, *, tm, tiles_per_seq, rows, final_norm):
    i = pl.program_id(0)
    j = pl.program_id(1)

    @pl.when(j == 0)
    def _():
        g = g_ref[...]
        shift = mod_ref[0, 3:4, :]
        scale = mod_ref[0, 4:5, :]
        first = (i % tiles_per_seq) == 0
        last = (i % tiles_per_seq) == tiles_per_seq - 1
        hp = _modulate(prev_ref[...], g, scale, shift)
        hn = _modulate(next_ref[...], g, scale, shift)
        h_scr[0:HALO, :] = jnp.where(first, 0.0, hp).astype(BF16)
        h_scr[HALO + tm:2 * HALO + tm, :] = jnp.where(last, 0.0, hn).astype(BF16)

        def body(r, carry):
            src = pl.ds(pl.multiple_of(r * rows, rows), rows)
            dst = pl.ds(pl.multiple_of(HALO + r * rows, HALO), rows)
            h_scr[dst, :] = _modulate(main_ref[src, :], g, scale, shift).astype(BF16)
            o_ref[src, :] = jnp.zeros((rows, o_ref.shape[1]), F32)
            return carry

        lax.fori_loop(0, tm // rows, body, 0)

    g_scr[...] = _dot(h_scr[...], wg_ref[...])
    val = _dot(h_scr[HALO:HALO + tm, :], wv_ref[...])
    conv = (cw_ref[0:1, :] * g_scr[HALO - 1:HALO - 1 + tm, :]
            + cw_ref[1:2, :] * g_scr[HALO:HALO + tm, :]
            + cw_ref[2:3, :] * g_scr[HALO + 1:HALO + 1 + tm, :]
            + cb_ref[...])
    act = (_silu(conv) * val).astype(BF16)
    o_ref[...] += _dot(act, wd_ref[...])

    @pl.when(j == pl.num_programs(1) - 1)
    def _():
        gate = mod_ref[0, 5:6, :]
        gf = gf_ref[...]

        def body(r, carry):
            sl = pl.ds(pl.multiple_of(r * rows, rows), rows)
            x2 = main_ref[sl, :] + gate * o_ref[sl, :]
            o_ref[sl, :] = _rms_rows(x2, gf) if final_norm else x2
            return carry

        lax.fori_loop(0, tm // rows, body, 0)


def _ffn(x1, mod, g_ffn, w_up, ffn_conv_w, ffn_conv_b, w_down, g_final, seq, final_norm,
         tm=512, tf=256):
    t, d = x1.shape
    f = w_down.shape[0]
    n_f = f // tf
    tiles_per_seq = seq // tm
    hb = tm // HALO
    n_hb = t // HALO
    cw_pad = jnp.pad(ffn_conv_w, ((0, (-ffn_conv_w.shape[0]) % 8), (0, 0)))
    kern = functools.partial(_ffn_kernel, tm=tm, tiles_per_seq=tiles_per_seq, rows=64,
                             final_norm=final_norm)
    return pl.pallas_call(
        kern,
        grid=(t // tm, n_f),
        in_specs=[
            pl.BlockSpec((HALO, d), lambda i, j: (jnp.maximum(i * hb - 1, 0), 0)),
            pl.BlockSpec((tm, d), lambda i, j: (i, 0)),
            pl.BlockSpec((HALO, d), lambda i, j: (jnp.minimum((i + 1) * hb, n_hb - 1), 0)),
            pl.BlockSpec((1, N_MOD_PAD, d), lambda i, j: (i // tiles_per_seq, 0, 0)),
            pl.BlockSpec((1, d), lambda i, j: (0, 0)),
            pl.BlockSpec((d, tf), lambda i, j: (0, j)),
            pl.BlockSpec((d, tf), lambda i, j: (0, j + n_f)),
            pl.BlockSpec((cw_pad.shape[0], tf), lambda i, j: (0, j)),
            pl.BlockSpec((1, tf), lambda i, j: (0, j)),
            pl.BlockSpec((tf, d), lambda i, j: (j, 0)),
            pl.BlockSpec((1, d), lambda i, j: (0, 0)),
        ],
        out_specs=pl.BlockSpec((tm, d), lambda i, j: (i, 0)),
        out_shape=jax.ShapeDtypeStruct((t, d), F32),
        scratch_shapes=[pltpu.VMEM((tm + 2 * HALO, d), BF16), pltpu.VMEM((tm + 2 * HALO, tf), F32)],
        compiler_params=_params(("arbitrary", "arbitrary")),
        name="ffn",
    )(x1, x1, x1, mod, g_ffn, w_up, w_up, cw_pad, ffn_conv_b, w_down, g_final)


def kernel(x, c, w_ada, b_ada, g_mix, w_in, conv_w, conv_b, conv_ln_g, conv_ln_b, sgu_ln_g, sgu_ln_b, sgu_w, sgu_b, out_g_conv, out_g_sgu, w_out, g_ffn, w_up, ffn_conv_w, ffn_conv_b, w_down, g_final):
    bsz, seq, d = x.shape
    depth = w_ada.shape[0]
    d_conv = conv_w.shape[2]
    n_mod = w_ada.shape[2] // d
    xf = x.reshape(bsz * seq, d)
    c_pad = jnp.pad(c, ((0, (-bsz) % 8), (0, 0)))

    for l in range(depth):
        mod = _ada(c_pad, w_ada[l], b_ada[l][None, :])[:bsz].reshape(bsz, n_mod, d)
        mod = jnp.pad(mod, ((0, 0), (0, N_MOD_PAD - n_mod), (0, 0)))

        a_glu, z = _inproj(xf, mod, g_mix[l][None, :], w_in[l].astype(BF16), d_conv, seq)
        ya = _convbr(a_glu, conv_w[l], conv_b[l][None, :], conv_ln_g[l][None, :],
                     conv_ln_b[l][None, :], out_g_conv[l][None, :], seq)
        bias_full = jnp.repeat(sgu_b[l].T, HEAD_DIM, axis=1)
        yb = _sgubr(z, sgu_w[l].astype(BF16), bias_full, sgu_ln_g[l][None, :],
                    sgu_ln_b[l][None, :], out_g_sgu[l][None, :])
        x1 = _outproj(ya, yb, w_out[l].astype(BF16), xf, mod, seq)
        xf = _ffn(x1, mod, g_ffn[l][None, :], w_up[l].astype(BF16), ffn_conv_w[l],
                  ffn_conv_b[l][None, :], w_down[l].astype(BF16), g_final[None, :], seq,
                  final_norm=(l == depth - 1))

    return xf.reshape(bsz, seq, d)
```

```python
import functools
import math

import jax
import jax.numpy as jnp
from jax import lax
from jax.experimental import pallas as pl
from jax.experimental.pallas import tpu as pltpu

EPS = 1e-6
CHUNK = 128
HEAD_DIM = 128
SUBLANE = 8
LANE = 128
HALO = 16
N_MOD_PAD = 8
VMEM_LIMIT_BYTES = 58 * 1024 * 1024

F32 = jnp.float32
BF16 = jnp.bfloat16


def _dot(a, b):
    return jnp.dot(a, b, preferred_element_type=F32)


def _silu(x):
    return x * jax.nn.sigmoid(x)


def _rms_rows(x, g):
    return x * lax.rsqrt(jnp.mean(x * x, axis=-1, keepdims=True) + EPS) * g


def _modulate(x, g, scale, shift):
    return _rms_rows(x, g) * (1.0 + scale) + shift


def _params(sem):
    return pltpu.CompilerParams(dimension_semantics=sem, vmem_limit_bytes=VMEM_LIMIT_BYTES)


def _ada_kernel(c_ref, w_ref, b_ref, o_ref):
    ca = _silu(c_ref[...]).astype(BF16)
    o_ref[...] = _dot(ca, w_ref[...].astype(BF16)) + b_ref[...]


def _ada(c_pad, w_ada, b_ada, tn=512):
    rows, d = c_pad.shape
    n = w_ada.shape[1]
    return pl.pallas_call(
        _ada_kernel,
        grid=(n // tn,),
        in_specs=[
            pl.BlockSpec((rows, d), lambda j: (0, 0)),
            pl.BlockSpec((d, tn), lambda j: (0, j)),
            pl.BlockSpec((1, tn), lambda j: (0, j)),
        ],
        out_specs=pl.BlockSpec((rows, tn), lambda j: (0, j)),
        out_shape=jax.ShapeDtypeStruct((rows, n), F32),
        compiler_params=_params(("arbitrary",)),
        name="ada",
    )(c_pad, w_ada, b_ada)


def _inproj_kernel(x_ref, mod_ref, g_ref, wm_ref, wg_ref, a_ref, z_ref, h_scr, *, n_glu, rows):
    j = pl.program_id(1)

    @pl.when(j == 0)
    def _():
        g = g_ref[...]
        shift = mod_ref[0, 0:1, :]
        scale = mod_ref[0, 1:2, :]

        def body(r, carry):
            sl = pl.ds(pl.multiple_of(r * rows, rows), rows)
            h_scr[sl, :] = _modulate(x_ref[sl, :], g, scale, shift).astype(BF16)
            return carry

        lax.fori_loop(0, x_ref.shape[0] // rows, body, 0)

    @pl.when(j < n_glu)
    def _():
        h = h_scr[...]
        a_ref[...] = _dot(h, wm_ref[...]) * jax.nn.sigmoid(_dot(h, wg_ref[...]))

    @pl.when(j >= n_glu)
    def _():
        p = _dot(h_scr[...], wm_ref[...])
        z_ref[...] = 0.5 * p * (1.0 + lax.erf(p * math.sqrt(0.5)))


def _inproj(x2d, mod, g_mix, w_in, d_conv, seq, tm=512, tn=512):
    t, d = x2d.shape
    n_glu = d_conv // tn
    n_z = (w_in.shape[1] - 2 * d_conv) // tn
    tiles_per_seq = seq // tm
    kern = functools.partial(_inproj_kernel, n_glu=n_glu, rows=64)
    return pl.pallas_call(
        kern,
        grid=(t // tm, n_glu + n_z),
        in_specs=[
            pl.BlockSpec((tm, d), lambda i, j: (i, 0)),
            pl.BlockSpec((1, N_MOD_PAD, d), lambda i, j: (i // tiles_per_seq, 0, 0)),
            pl.BlockSpec((1, d), lambda i, j: (0, 0)),
            pl.BlockSpec((d, tn), lambda i, j: (0, j + jnp.where(j >= n_glu, n_glu, 0))),
            pl.BlockSpec((d, tn), lambda i, j: (0, jnp.minimum(j + n_glu, 2 * n_glu - 1))),
        ],
        out_specs=[
            pl.BlockSpec((tm, tn), lambda i, j: (i, jnp.minimum(j, n_glu - 1))),
            pl.BlockSpec((tm, tn), lambda i, j: (i, jnp.maximum(j - n_glu, 0))),
        ],
        out_shape=[
            jax.ShapeDtypeStruct((t, d_conv), F32),
            jax.ShapeDtypeStruct((t, n_z * tn), F32),
        ],
        scratch_shapes=[pltpu.VMEM((tm, d), BF16)],
        compiler_params=_params(("arbitrary", "arbitrary")),
        name="inproj",
    )(x2d, mod, g_mix, w_in, w_in)


def _convbr_kernel(prev_ref, main_ref, next_ref, w_ref, cb_ref, lg_ref, lb_ref, og_ref, y_ref,
                   buf, cbuf, *, ts, tiles_per_seq, taps, rb, norm_rows, norm_unroll):
    i = pl.program_id(0)
    first = (i % tiles_per_seq) == 0
    last = (i % tiles_per_seq) == tiles_per_seq - 1
    buf[0:HALO, :] = jnp.where(first, 0.0, prev_ref[...])
    buf[HALO:HALO + ts, :] = main_ref[...]
    buf[HALO + ts:2 * HALO + ts, :] = jnp.where(last, 0.0, next_ref[...])
    c = main_ref.shape[1]
    off = HALO - (taps - 1) // 2
    n_lane = c // LANE
    n_out = rb // SUBLANE
    n_win = n_out + 2 * HALO // SUBLANE
    row_iota = lax.broadcasted_iota(jnp.int32, (SUBLANE, LANE), 0)

    def conv_body(t, carry):
        base = pl.multiple_of((t // n_lane) * rb, rb)
        lane = pl.ds(pl.multiple_of((t % n_lane) * LANE, LANE), LANE)
        win = [buf[pl.ds(base + SUBLANE * m, SUBLANE), lane] for m in range(n_win)]
        acc = [cb_ref[:, lane]] * n_out
        for rho in range(SUBLANE):
            if rho == 0:
                sh = win
            else:
                rolled = [pltpu.roll(w, SUBLANE - rho, 0) for w in win]
                lower = row_iota < (SUBLANE - rho)
                sh = [jnp.where(lower, rolled[m], rolled[m + 1]) for m in range(n_win - 1)]
            for q in range(2 * HALO // SUBLANE):
                k = SUBLANE * q + rho - off
                if 0 <= k < taps:
                    wv = w_ref[k, :, lane]
                    acc = [acc[m] + wv * sh[m + q] for m in range(n_out)]
        for m in range(n_out):
            cbuf[pl.ds(base + SUBLANE * m, SUBLANE), lane] = acc[m]
        return carry

    lax.fori_loop(0, (ts // rb) * n_lane, conv_body, 0)

    lg = lg_ref[...]
    lb = lb_ref[...]
    og = og_ref[...]

    def norm_body(r, carry):
        sl = pl.ds(pl.multiple_of(r * norm_rows, norm_rows), norm_rows)
        a = cbuf[sl, :]
        xc = a - jnp.mean(a, axis=-1, keepdims=True)
        var = jnp.mean(xc * xc, axis=-1, keepdims=True)
        s = _silu(xc * lax.rsqrt(var + EPS) * lg + lb)
        y_ref[sl, :] = _rms_rows(s, og).astype(BF16)
        return carry

    lax.fori_loop(0, ts // norm_rows, norm_body, 0, unroll=norm_unroll)


def _convbr(a_glu, conv_w, conv_b, ln_g, ln_b, out_g, seq, ts=256):
    t, c = a_glu.shape
    taps = conv_w.shape[0]
    w_b = jnp.broadcast_to(conv_w[:, None, :], (taps, SUBLANE, c))
    cb_b = jnp.broadcast_to(conv_b, (SUBLANE, c))
    tiles_per_seq = seq // ts
    hb = ts // HALO
    n_hb = t // HALO
    kern = functools.partial(_convbr_kernel, ts=ts, tiles_per_seq=tiles_per_seq, taps=taps,
                             rb=64, norm_rows=16, norm_unroll=4)
    row = lambda i: (0, 0)
    return pl.pallas_call(
        kern,
        grid=(t // ts,),
        in_specs=[
            pl.BlockSpec((HALO, c), lambda i: (jnp.maximum(i * hb - 1, 0), 0)),
            pl.BlockSpec((ts, c), lambda i: (i, 0)),
            pl.BlockSpec((HALO, c), lambda i: (jnp.minimum((i + 1) * hb, n_hb - 1), 0)),
            pl.BlockSpec(w_b.shape, lambda i: (0, 0, 0)),
            pl.BlockSpec((SUBLANE, c), row),
            pl.BlockSpec((1, c), row),
            pl.BlockSpec((1, c), row),
            pl.BlockSpec((1, c), row),
        ],
        out_specs=pl.BlockSpec((ts, c), lambda i: (i, 0)),
        out_shape=jax.ShapeDtypeStruct((t, c), BF16),
        scratch_shapes=[pltpu.VMEM((ts + 2 * HALO, c), F32), pltpu.VMEM((ts, c), F32)],
        compiler_params=_params(("arbitrary",)),
        name="convbr",
    )(a_glu, a_glu, a_glu, w_b, cb_b, ln_g, ln_b, out_g)


def _sgubr_kernel(u_ref, v_ref, w_ref, bias_ref, lg_ref, lb_ref, og_ref, y_ref, vn_scr, b_scr,
                  *, ts, rows):
    lg = lg_ref[...]
    lb = lb_ref[...]
    og = og_ref[...]
    n_chunks = ts // CHUNK
    n_heads = v_ref.shape[1] // HEAD_DIM

    def ln_body(r, carry):
        sl = pl.ds(pl.multiple_of(r * rows, rows), rows)
        v = v_ref[sl, :]
        xc = v - jnp.mean(v, axis=-1, keepdims=True)
        var = jnp.mean(xc * xc, axis=-1, keepdims=True)
        vn_scr[sl, :] = (xc * lax.rsqrt(var + EPS) * lg + lb).astype(BF16)
        return carry

    lax.fori_loop(0, ts // rows, ln_body, 0, unroll=4)

    for h in range(n_heads):
        hs = slice(h * HEAD_DIM, (h + 1) * HEAD_DIM)
        rhs = jnp.concatenate(
            [vn_scr[ck * CHUNK:(ck + 1) * CHUNK, hs] for ck in range(n_chunks)], axis=1)
        mixed = _dot(w_ref[h], rhs)
        bias = bias_ref[:, hs]
        for ck in range(n_chunks):
            rs = slice(ck * CHUNK, (ck + 1) * CHUNK)
            m = mixed[:, ck * HEAD_DIM:(ck + 1) * HEAD_DIM] + bias
            b_scr[rs, hs] = u_ref[rs, hs] * m

    def rms_body(r, carry):
        sl = pl.ds(pl.multiple_of(r * rows, rows), rows)
        y_ref[sl, :] = _rms_rows(b_scr[sl, :], og).astype(BF16)
        return carry

    lax.fori_loop(0, ts // rows, rms_body, 0, unroll=4)


def _sgubr(z, sgu_w, bias_full, ln_g, ln_b, out_g, ts=256):
    t, two_c = z.shape
    c = two_c // 2
    kern = functools.partial(_sgubr_kernel, ts=ts, rows=16)
    row = lambda i: (0, 0)
    return pl.pallas_call(
        kern,
        grid=(t // ts,),
        in_specs=[
            pl.BlockSpec((ts, c), lambda i: (i, 0)),
            pl.BlockSpec((ts, c), lambda i: (i, 1)),
            pl.BlockSpec(sgu_w.shape, lambda i: (0, 0, 0)),
            pl.BlockSpec(bias_full.shape, row),
            pl.BlockSpec((1, c), row),
            pl.BlockSpec((1, c), row),
            pl.BlockSpec((1, c), row),
        ],
        out_specs=pl.BlockSpec((ts, c), lambda i: (i, 0)),
        out_shape=jax.ShapeDtypeStruct((t, c), BF16),
        scratch_shapes=[pltpu.VMEM((ts, c), BF16), pltpu.VMEM((ts, c), F32)],
        compiler_params=_params(("arbitrary",)),
        name="sgubr",
    )(z, z, sgu_w, bias_full, ln_g, ln_b, out_g)


def _outproj_kernel(ya_ref, yb_ref, wa_ref, wb_ref, x_ref, mod_ref, o_ref):
    y = _dot(ya_ref[...], wa_ref[...]) + _dot(yb_ref[...], wb_ref[...])
    o_ref[...] = x_ref[...] + mod_ref[0, 2:3, :] * y


def _outproj(ya, yb, w_out, x2d, mod, seq, tm=512, tn=512):
    t, c = ya.shape
    d = x2d.shape[1]
    tiles_per_seq = seq // tm
    return pl.pallas_call(
        _outproj_kernel,
        grid=(t // tm, d // tn),
        in_specs=[
            pl.BlockSpec((tm, c), lambda i, j: (i, 0)),
            pl.BlockSpec((tm, c), lambda i, j: (i, 0)),
            pl.BlockSpec((c, tn), lambda i, j: (0, j)),
            pl.BlockSpec((c, tn), lambda i, j: (1, j)),
            pl.BlockSpec((tm, tn), lambda i, j: (i, j)),
            pl.BlockSpec((1, N_MOD_PAD, tn), lambda i, j: (i // tiles_per_seq, 0, j)),
        ],
        out_specs=pl.BlockSpec((tm, tn), lambda i, j: (i, j)),
        out_shape=jax.ShapeDtypeStruct((t, d), F32),
        compiler_params=_params(("arbitrary", "arbitrary")),
        name="outproj",
    )(ya, yb, w_out, w_out, x2d, mod)


def _ffn_kernel(prev_ref, main_ref, next_ref, mod_ref, g_ref, wg_ref, wv_ref, cw_ref, cb_ref,
                wd_ref, gf_ref, o_ref, h_scr, g_scr, *, tm, tiles_per_seq, rows, final_norm):
    i = pl.program_id(0)
    j = pl.program_id(1)

    @pl.when(j == 0)
    def _():
        g = g_ref[...]
        shift = mod_ref[0, 3:4, :]
        scale = mod_ref[0, 4:5, :]
        first = (i % tiles_per_seq) == 0
        last = (i % tiles_per_seq) == tiles_per_seq - 1
        hp = _modulate(prev_ref[...], g, scale, shift)
        hn = _modulate(next_ref[...], g, scale, shift)
        h_scr[0:HALO, :] = jnp.where(first, 0.0, hp).astype(BF16)
        h_scr[HALO + tm:2 * HALO + tm, :] = jnp.where(last, 0.0, hn).astype(BF16)

        def body(r, carry):
            src = pl.ds(pl.multiple_of(r * rows, rows), rows)
            dst = pl.ds(pl.multiple_of(HALO + r * rows, HALO), rows)
            h_scr[dst, :] = _modulate(main_ref[src, :], g, scale, shift).astype(BF16)
            o_ref[src, :] = jnp.zeros((rows, o_ref.shape[1]), F32)
            return carry

        lax.fori_loop(0, tm // rows, body, 0)

    g_scr[...] = _dot(h_scr[...], wg_ref[...])
    val = _dot(h_scr[HALO:HALO + tm, :], wv_ref[...])
    conv = (cw_ref[0:1, :] * g_scr[HALO - 1:HALO - 1 + tm, :]
            + cw_ref[1:2, :] * g_scr[HALO:HALO + tm, :]
            + cw_ref[2:3, :] * g_scr[HALO + 1:HALO + 1 + tm, :]
            + cb_ref[...])
    act = (_silu(conv) * val).astype(BF16)
    o_ref[...] += _dot(act, wd_ref[...])

    @pl.when(j == pl.num_programs(1) - 1)
    def _():
        gate = mod_ref[0, 5:6, :]
        gf = gf_ref[...]

        def body(r, carry):
            sl = pl.ds(pl.multiple_of(r * rows, rows), rows)
            x2 = main_ref[sl, :] + gate * o_ref[sl, :]
            o_ref[sl, :] = _rms_rows(x2, gf) if final_norm else x2
            return carry

        lax.fori_loop(0, tm // rows, body, 0)


def _ffn(x1, mod, g_ffn, w_up, ffn_conv_w, ffn_conv_b, w_down, g_final, seq, final_norm,
         tm=1024, tf=256):
    t, d = x1.shape
    f = w_down.shape[0]
    n_f = f // tf
    tiles_per_seq = seq // tm
    hb = tm // HALO
    n_hb = t // HALO
    cw_pad = jnp.pad(ffn_conv_w, ((0, (-ffn_conv_w.shape[0]) % 8), (0, 0)))
    kern = functools.partial(_ffn_kernel, tm=tm, tiles_per_seq=tiles_per_seq, rows=64,
                             final_norm=final_norm)
    return pl.pallas_call(
        kern,
        grid=(t // tm, n_f),
        in_specs=[
            pl.BlockSpec((HALO, d), lambda i, j: (jnp.maximum(i * hb - 1, 0), 0)),
            pl.BlockSpec((tm, d), lambda i, j: (i, 0), pipeline_mode=pl.Buffered(1)),
            pl.BlockSpec((HALO, d), lambda i, j: (jnp.minimum((i + 1) * hb, n_hb - 1), 0)),
            pl.BlockSpec((1, N_MOD_PAD, d), lambda i, j: (i // tiles_per_seq, 0, 0)),
            pl.BlockSpec((1, d), lambda i, j: (0, 0)),
            pl.BlockSpec((d, tf), lambda i, j: (0, j)),
            pl.BlockSpec((d, tf), lambda i, j: (0, j + n_f)),
            pl.BlockSpec((cw_pad.shape[0], tf), lambda i, j: (0, j)),
            pl.BlockSpec((1, tf), lambda i, j: (0, j)),
            pl.BlockSpec((tf, d), lambda i, j: (j, 0)),
            pl.BlockSpec((1, d), lambda i, j: (0, 0)),
        ],
        out_specs=pl.BlockSpec((tm, d), lambda i, j: (i, 0), pipeline_mode=pl.Buffered(1)),
        out_shape=jax.ShapeDtypeStruct((t, d), F32),
        scratch_shapes=[pltpu.VMEM((tm + 2 * HALO, d), BF16), pltpu.VMEM((tm + 2 * HALO, tf), F32)],
        compiler_params=_params(("arbitrary", "arbitrary")),
        name="ffn",
    )(x1, x1, x1, mod, g_ffn, w_up, w_up, cw_pad, ffn_conv_b, w_down, g_final)


def kernel(x, c, w_ada, b_ada, g_mix, w_in, conv_w, conv_b, conv_ln_g, conv_ln_b, sgu_ln_g, sgu_ln_b, sgu_w, sgu_b, out_g_conv, out_g_sgu, w_out, g_ffn, w_up, ffn_conv_w, ffn_conv_b, w_down, g_final):
    bsz, seq, d = x.shape
    depth = w_ada.shape[0]
    d_conv = conv_w.shape[2]
    n_mod = w_ada.shape[2] // d
    xf = x.reshape(bsz * seq, d)
    c_pad = jnp.pad(c, ((0, (-bsz) % 8), (0, 0)))

    for l in range(depth):
        mod = _ada(c_pad, w_ada[l], b_ada[l][None, :])[:bsz].reshape(bsz, n_mod, d)
        mod = jnp.pad(mod, ((0, 0), (0, N_MOD_PAD - n_mod), (0, 0)))

        a_glu, z = _inproj(xf, mod, g_mix[l][None, :], w_in[l].astype(BF16), d_conv, seq)
        ya = _convbr(a_glu, conv_w[l], conv_b[l][None, :], conv_ln_g[l][None, :],
                     conv_ln_b[l][None, :], out_g_conv[l][None, :], seq)
        bias_full = jnp.repeat(sgu_b[l].T, HEAD_DIM, axis=1)
        yb = _sgubr(z, sgu_w[l].astype(BF16), bias_full, sgu_ln_g[l][None, :],
                    sgu_ln_b[l][None, :], out_g_sgu[l][None, :])
        x1 = _outproj(ya, yb, w_out[l].astype(BF16), xf, mod, seq)
        xf = _ffn(x1, mod, g_ffn[l][None, :], w_up[l].astype(BF16), ffn_conv_w[l],
                  ffn_conv_b[l][None, :], w_down[l].astype(BF16), g_final[None, :], seq,
                  final_norm=(l == depth - 1))

    return xf.reshape(bsz, seq, d)
```

```python
import functools
import math

import jax
import jax.numpy as jnp
from jax import lax
from jax.experimental import pallas as pl
from jax.experimental.pallas import tpu as pltpu

EPS = 1e-6
CHUNK = 128
HEAD_DIM = 128
SUBLANE = 8
LANE = 128
HALO = 16
N_MOD_PAD = 8
VMEM_LIMIT_BYTES = 62 * 1024 * 1024

F32 = jnp.float32
BF16 = jnp.bfloat16


def _dot(a, b):
    return jnp.dot(a, b, preferred_element_type=F32)


def _sigmoid(x):
    return 0.5 * (1.0 + jnp.tanh(0.5 * x))


def _silu(x):
    return x * _sigmoid(x)


def _rms_rows(x, g):
    return x * lax.rsqrt(jnp.mean(x * x, axis=-1, keepdims=True) + EPS) * g


def _modulate(x, g, scale, shift):
    return _rms_rows(x, g) * (1.0 + scale) + shift


def _cast_block(src_ref, dst_ref, lanes=512):
    def body(c, carry):
        sl = pl.ds(pl.multiple_of(c * lanes, lanes), lanes)
        dst_ref[:, sl] = src_ref[:, sl].astype(BF16)
        return carry

    lax.fori_loop(0, src_ref.shape[1] // lanes, body, 0)


def _params(sem):
    return pltpu.CompilerParams(dimension_semantics=sem, vmem_limit_bytes=VMEM_LIMIT_BYTES)


def _ada_kernel(c_ref, w_ref, b_ref, o_ref):
    ca = _silu(c_ref[...]).astype(BF16)
    o_ref[...] = _dot(ca, w_ref[...].astype(BF16)) + b_ref[...]


def _ada(c_pad, w_ada, b_ada, tn=512):
    rows, d = c_pad.shape
    n = w_ada.shape[1]
    return pl.pallas_call(
        _ada_kernel,
        grid=(n // tn,),
        in_specs=[
            pl.BlockSpec((rows, d), lambda j: (0, 0)),
            pl.BlockSpec((d, tn), lambda j: (0, j)),
            pl.BlockSpec((1, tn), lambda j: (0, j)),
        ],
        out_specs=pl.BlockSpec((rows, tn), lambda j: (0, j)),
        out_shape=jax.ShapeDtypeStruct((rows, n), F32),
        compiler_params=_params(("arbitrary",)),
        name="ada",
    )(c_pad, w_ada, b_ada)


def _inproj_kernel(x_ref, mod_ref, g_ref, wm_ref, wg_ref, wo_ref, wd_ref, a_ref, z_ref, wob_ref,
                   wdb_ref, h_scr, *, n_glu, rows):
    j = pl.program_id(1)
    _cast_block(wo_ref, wob_ref)
    _cast_block(wd_ref, wdb_ref)

    @pl.when(j == 0)
    def _():
        g = g_ref[...]
        shift = mod_ref[0, 0:1, :]
        scale = mod_ref[0, 1:2, :]

        def body(r, carry):
            sl = pl.ds(pl.multiple_of(r * rows, rows), rows)
            h_scr[sl, :] = _modulate(x_ref[sl, :], g, scale, shift).astype(BF16)
            return carry

        lax.fori_loop(0, x_ref.shape[0] // rows, body, 0)

    @pl.when(j < n_glu)
    def _():
        h = h_scr[...]
        a_ref[...] = _dot(h, wm_ref[...]) * _sigmoid(_dot(h, wg_ref[...]))

    @pl.when(j >= n_glu)
    def _():
        p = _dot(h_scr[...], wm_ref[...])
        z_ref[...] = 0.5 * p * (1.0 + lax.erf(p * math.sqrt(0.5)))


def _cast_spec(w, n_steps, step):
    rows = -(-w.shape[0] // n_steps // HALO) * HALO
    last = -(-w.shape[0] // rows) - 1
    return pl.BlockSpec((rows, w.shape[1]), lambda *ids: (jnp.minimum(step(*ids), last), 0))


def _inproj(x2d, mod, g_mix, w_in, w_out, w_down, d_conv, seq, tm=512, tn=512):
    t, d = x2d.shape
    n_glu = d_conv // tn
    n_z = (w_in.shape[1] - 2 * d_conv) // tn
    n_j = n_glu + n_z
    n_steps = (t // tm) * n_j
    wo_spec = _cast_spec(w_out, n_steps, lambda i, j: i * n_j + j)
    wd_spec = _cast_spec(w_down, n_steps, lambda i, j: i * n_j + j)
    w_bf = w_in.astype(BF16)
    tiles_per_seq = seq // tm
    kern = functools.partial(_inproj_kernel, n_glu=n_glu, rows=64)
    return pl.pallas_call(
        kern,
        grid=(t // tm, n_glu + n_z),
        in_specs=[
            pl.BlockSpec((tm, d), lambda i, j: (i, 0)),
            pl.BlockSpec((1, N_MOD_PAD, d), lambda i, j: (i // tiles_per_seq, 0, 0)),
            pl.BlockSpec((1, d), lambda i, j: (0, 0)),
            pl.BlockSpec((d, tn), lambda i, j: (0, j + jnp.where(j >= n_glu, n_glu, 0))),
            pl.BlockSpec((d, tn), lambda i, j: (0, jnp.minimum(j + n_glu, 2 * n_glu - 1))),
            wo_spec,
            wd_spec,
        ],
        out_specs=[
            pl.BlockSpec((tm, tn), lambda i, j: (i, jnp.minimum(j, n_glu - 1))),
            pl.BlockSpec((tm, tn), lambda i, j: (i, jnp.maximum(j - n_glu, 0))),
            wo_spec,
            wd_spec,
        ],
        out_shape=[
            jax.ShapeDtypeStruct((t, d_conv), F32),
            jax.ShapeDtypeStruct((t, n_z * tn), F32),
            jax.ShapeDtypeStruct(w_out.shape, BF16),
            jax.ShapeDtypeStruct(w_down.shape, BF16),
        ],
        scratch_shapes=[pltpu.VMEM((tm, d), BF16)],
        compiler_params=_params(("arbitrary", "arbitrary")),
        name="inproj",
    )(x2d, mod, g_mix, w_bf, w_bf, w_out, w_down)


def _convbr_kernel(prev_ref, main_ref, next_ref, w_ref, cb_ref, lg_ref, lb_ref, og_ref, wu_ref,
                   y_ref, wub_ref, buf, cbuf, *, ts, tiles_per_seq, taps, rb, norm_rows,
                   norm_unroll):
    i = pl.program_id(0)
    _cast_block(wu_ref, wub_ref)
    first = (i % tiles_per_seq) == 0
    last = (i % tiles_per_seq) == tiles_per_seq - 1
    buf[0:HALO, :] = jnp.where(first, 0.0, prev_ref[...])
    buf[HALO:HALO + ts, :] = main_ref[...]
    buf[HALO + ts:2 * HALO + ts, :] = jnp.where(last, 0.0, next_ref[...])
    c = main_ref.shape[1]
    off = HALO - (taps - 1) // 2
    n_lane = c // LANE
    n_out = rb // SUBLANE
    n_win = n_out + 2 * HALO // SUBLANE
    row_iota = lax.broadcasted_iota(jnp.int32, (SUBLANE, LANE), 0)

    def conv_body(t, carry):
        base = pl.multiple_of((t // n_lane) * rb, rb)
        lane = pl.ds(pl.multiple_of((t % n_lane) * LANE, LANE), LANE)
        win = [buf[pl.ds(base + SUBLANE * m, SUBLANE), lane] for m in range(n_win)]
        acc = [cb_ref[:, lane]] * n_out
        for rho in range(SUBLANE):
            if rho == 0:
                sh = win
            else:
                rolled = [pltpu.roll(w, SUBLANE - rho, 0) for w in win]
                lower = row_iota < (SUBLANE - rho)
                sh = [jnp.where(lower, rolled[m], rolled[m + 1]) for m in range(n_win - 1)]
            for q in range(2 * HALO // SUBLANE):
                k = SUBLANE * q + rho - off
                if 0 <= k < taps:
                    wv = w_ref[k, :, lane]
                    acc = [acc[m] + wv * sh[m + q] for m in range(n_out)]
        for m in range(n_out):
            cbuf[pl.ds(base + SUBLANE * m, SUBLANE), lane] = acc[m]
        return carry

    lax.fori_loop(0, (ts // rb) * n_lane, conv_body, 0)

    lg = lg_ref[...]
    lb = lb_ref[...]
    og = og_ref[...]

    def norm_body(r, carry):
        sl = pl.ds(pl.multiple_of(r * norm_rows, norm_rows), norm_rows)
        a = cbuf[sl, :]
        xc = a - jnp.mean(a, axis=-1, keepdims=True)
        var = jnp.mean(xc * xc, axis=-1, keepdims=True)
        s = _silu(xc * lax.rsqrt(var + EPS) * lg + lb)
        y_ref[sl, :] = _rms_rows(s, og).astype(BF16)
        return carry

    lax.fori_loop(0, ts // norm_rows, norm_body, 0, unroll=norm_unroll)


def _convbr(a_glu, conv_w, conv_b, ln_g, ln_b, out_g, w_up, seq, ts=256):
    t, c = a_glu.shape
    wu_rows = w_up.shape[0] // (t // ts)
    assert wu_rows * (t // ts) == w_up.shape[0] and wu_rows % HALO == 0
    taps = conv_w.shape[0]
    w_b = jnp.broadcast_to(conv_w[:, None, :], (taps, SUBLANE, c))
    cb_b = jnp.broadcast_to(conv_b, (SUBLANE, c))
    tiles_per_seq = seq // ts
    hb = ts // HALO
    n_hb = t // HALO
    kern = functools.partial(_convbr_kernel, ts=ts, tiles_per_seq=tiles_per_seq, taps=taps,
                             rb=64, norm_rows=16, norm_unroll=4)
    row = lambda i: (0, 0)
    return pl.pallas_call(
        kern,
        grid=(t // ts,),
        in_specs=[
            pl.BlockSpec((HALO, c), lambda i: (jnp.maximum(i * hb - 1, 0), 0)),
            pl.BlockSpec((ts, c), lambda i: (i, 0)),
            pl.BlockSpec((HALO, c), lambda i: (jnp.minimum((i + 1) * hb, n_hb - 1), 0)),
            pl.BlockSpec(w_b.shape, lambda i: (0, 0, 0)),
            pl.BlockSpec((SUBLANE, c), row),
            pl.BlockSpec((1, c), row),
            pl.BlockSpec((1, c), row),
            pl.BlockSpec((1, c), row),
            pl.BlockSpec((wu_rows, w_up.shape[1]), lambda i: (i, 0)),
        ],
        out_specs=[pl.BlockSpec((ts, c), lambda i: (i, 0)),
                   pl.BlockSpec((wu_rows, w_up.shape[1]), lambda i: (i, 0))],
        out_shape=[jax.ShapeDtypeStruct((t, c), BF16),
                   jax.ShapeDtypeStruct(w_up.shape, BF16)],
        scratch_shapes=[pltpu.VMEM((ts + 2 * HALO, c), F32), pltpu.VMEM((ts, c), F32)],
        compiler_params=_params(("arbitrary",)),
        name="convbr",
    )(a_glu, a_glu, a_glu, w_b, cb_b, ln_g, ln_b, out_g, w_up)


def _sgubr_kernel(u_ref, v_ref, w_ref, bias_ref, lg_ref, lb_ref, og_ref, y_ref, vn_scr, b_scr,
                  *, ts, rows):
    lg = lg_ref[...]
    lb = lb_ref[...]
    og = og_ref[...]
    n_chunks = ts // CHUNK
    n_heads = v_ref.shape[1] // HEAD_DIM

    def ln_body(r, carry):
        sl = pl.ds(pl.multiple_of(r * rows, rows), rows)
        v = v_ref[sl, :]
        xc = v - jnp.mean(v, axis=-1, keepdims=True)
        var = jnp.mean(xc * xc, axis=-1, keepdims=True)
        vn_scr[sl, :] = (xc * lax.rsqrt(var + EPS) * lg + lb).astype(BF16)
        return carry

    lax.fori_loop(0, ts // rows, ln_body, 0, unroll=4)

    for h in range(n_heads):
        hs = slice(h * HEAD_DIM, (h + 1) * HEAD_DIM)
        rhs = jnp.concatenate(
            [vn_scr[ck * CHUNK:(ck + 1) * CHUNK, hs] for ck in range(n_chunks)], axis=1)
        mixed = _dot(w_ref[h], rhs)
        bias = bias_ref[:, hs]
        for ck in range(n_chunks):
            rs = slice(ck * CHUNK, (ck + 1) * CHUNK)
            m = mixed[:, ck * HEAD_DIM:(ck + 1) * HEAD_DIM] + bias
            b_scr[rs, hs] = u_ref[rs, hs] * m

    def rms_body(r, carry):
        sl = pl.ds(pl.multiple_of(r * rows, rows), rows)
        y_ref[sl, :] = _rms_rows(b_scr[sl, :], og).astype(BF16)
        return carry

    lax.fori_loop(0, ts // rows, rms_body, 0, unroll=4)


def _sgubr(z, sgu_w, bias_full, ln_g, ln_b, out_g, ts=256):
    t, two_c = z.shape
    c = two_c // 2
    kern = functools.partial(_sgubr_kernel, ts=ts, rows=16)
    row = lambda i: (0, 0)
    return pl.pallas_call(
        kern,
        grid=(t // ts,),
        in_specs=[
            pl.BlockSpec((ts, c), lambda i: (i, 0)),
            pl.BlockSpec((ts, c), lambda i: (i, 1)),
            pl.BlockSpec(sgu_w.shape, lambda i: (0, 0, 0)),
            pl.BlockSpec(bias_full.shape, row),
            pl.BlockSpec((1, c), row),
            pl.BlockSpec((1, c), row),
            pl.BlockSpec((1, c), row),
        ],
        out_specs=pl.BlockSpec((ts, c), lambda i: (i, 0)),
        out_shape=jax.ShapeDtypeStruct((t, c), BF16),
        scratch_shapes=[pltpu.VMEM((ts, c), BF16), pltpu.VMEM((ts, c), F32)],
        compiler_params=_params(("arbitrary",)),
        name="sgubr",
    )(z, z, sgu_w, bias_full, ln_g, ln_b, out_g)


def _outproj_kernel(ya_ref, yb_ref, wa_ref, wb_ref, x_ref, mod_ref, o_ref):
    y = _dot(ya_ref[...], wa_ref[...]) + _dot(yb_ref[...], wb_ref[...])
    o_ref[...] = x_ref[...] + mod_ref[0, 2:3, :] * y


def _outproj(ya, yb, w_out, x2d, mod, seq, tm=1024, tn=1024):
    t, c = ya.shape
    d = x2d.shape[1]
    tiles_per_seq = seq // tm
    return pl.pallas_call(
        _outproj_kernel,
        grid=(t // tm, d // tn),
        in_specs=[
            pl.BlockSpec((tm, c), lambda i, j: (i, 0)),
            pl.BlockSpec((tm, c), lambda i, j: (i, 0)),
            pl.BlockSpec((c, tn), lambda i, j: (0, j)),
            pl.BlockSpec((c, tn), lambda i, j: (1, j)),
            pl.BlockSpec((tm, tn), lambda i, j: (i, j)),
            pl.BlockSpec((1, N_MOD_PAD, tn), lambda i, j: (i // tiles_per_seq, 0, j)),
        ],
        out_specs=pl.BlockSpec((tm, tn), lambda i, j: (i, j)),
        out_shape=jax.ShapeDtypeStruct((t, d), F32),
        compiler_params=_params(("arbitrary", "arbitrary")),
        name="outproj",
    )(ya, yb, w_out, w_out, x2d, mod)


def _ffn_kernel(prev_ref, main_ref, next_ref, mod_ref, g_ref, wg_ref, wv_ref, cw_ref, cb_ref,
                wd_ref, gf_ref, o_ref, h_scr, g_scr, act0_scr, act1_scr, *, tm, tiles_per_seq,
                rows, final_norm):
    act_scr = (act0_scr, act1_scr)
    i = pl.program_id(0)
    j = pl.program_id(1)

    @pl.when(j == 0)
    def _():
        g = g_ref[...]
        shift = mod_ref[0, 3:4, :]
        scale = mod_ref[0, 4:5, :]
        first = (i % tiles_per_seq) == 0
        last = (i % tiles_per_seq) == tiles_per_seq - 1
        hp = _modulate(prev_ref[...], g, scale, shift)
        hn = _modulate(next_ref[...], g, scale, shift)
        h_scr[0:HALO, :] = jnp.where(first, 0.0, hp).astype(BF16)
        h_scr[HALO + tm:2 * HALO + tm, :] = jnp.where(last, 0.0, hn).astype(BF16)

        def body(r, carry):
            src = pl.ds(pl.multiple_of(r * rows, rows), rows)
            dst = pl.ds(pl.multiple_of(HALO + r * rows, HALO), rows)
            h_scr[dst, :] = _modulate(main_ref[src, :], g, scale, shift).astype(BF16)
            o_ref[src, :] = jnp.zeros((rows, o_ref.shape[1]), F32)
            return carry

        lax.fori_loop(0, tm // rows, body, 0)

    def up_and_act(slot):
        g_scr[...] = _dot(h_scr[...], wg_ref[...])
        val = _dot(h_scr[HALO:HALO + tm, :], wv_ref[...])
        conv = (cw_ref[0:1, :] * g_scr[HALO - 1:HALO - 1 + tm, :]
                + cw_ref[1:2, :] * g_scr[HALO:HALO + tm, :]
                + cw_ref[2:3, :] * g_scr[HALO + 1:HALO + 1 + tm, :]
                + cb_ref[...])
        act_scr[slot][...] = (_silu(conv) * val).astype(BF16)

    def down_proj(slot):
        o_ref[...] += _dot(act_scr[slot][...], wd_ref[...])

    n_f = pl.num_programs(1) - 1
    middle = jnp.logical_and(j > 0, j < n_f)

    @pl.when(j == 0)
    def _():
        up_and_act(0)

    for parity in range(2):
        @pl.when(jnp.logical_and(middle, j % 2 == parity))
        def _():
            up_and_act(parity)
            down_proj(1 - parity)

        @pl.when(jnp.logical_and(j == n_f, j % 2 == parity))
        def _():
            down_proj(1 - parity)

    @pl.when(j == pl.num_programs(1) - 1)
    def _():
        gate = mod_ref[0, 5:6, :]
        gf = gf_ref[...]

        def body(r, carry):
            sl = pl.ds(pl.multiple_of(r * rows, rows), rows)
            x2 = main_ref[sl, :] + gate * o_ref[sl, :]
            o_ref[sl, :] = _rms_rows(x2, gf) if final_norm else x2
            return carry

        lax.fori_loop(0, tm // rows, body, 0)


def _ffn(x1, mod, g_ffn, w_up, ffn_conv_w, ffn_conv_b, w_down, g_final, seq, final_norm,
         tm=1024, tf=256):
    t, d = x1.shape
    f = w_down.shape[0]
    n_f = f // tf
    tiles_per_seq = seq // tm
    hb = tm // HALO
    n_hb = t // HALO
    cw_pad = jnp.pad(ffn_conv_w, ((0, (-ffn_conv_w.shape[0]) % 8), (0, 0)))
    kern = functools.partial(_ffn_kernel, tm=tm, tiles_per_seq=tiles_per_seq, rows=64,
                             final_norm=final_norm)
    up = lambda j: jnp.minimum(j, n_f - 1)
    down = lambda j: jnp.maximum(j - 1, 0)
    return pl.pallas_call(
        kern,
        grid=(t // tm, n_f + 1),
        in_specs=[
            pl.BlockSpec((HALO, d), lambda i, j: (jnp.maximum(i * hb - 1, 0), 0)),
            pl.BlockSpec((tm, d), lambda i, j: (i, 0), pipeline_mode=pl.Buffered(1)),
            pl.BlockSpec((HALO, d), lambda i, j: (jnp.minimum((i + 1) * hb, n_hb - 1), 0)),
            pl.BlockSpec((1, N_MOD_PAD, d), lambda i, j: (i // tiles_per_seq, 0, 0)),
            pl.BlockSpec((1, d), lambda i, j: (0, 0)),
            pl.BlockSpec((d, tf), lambda i, j: (0, up(j))),
            pl.BlockSpec((d, tf), lambda i, j: (0, up(j) + n_f)),
            pl.BlockSpec((cw_pad.shape[0], tf), lambda i, j: (0, up(j))),
            pl.BlockSpec((1, tf), lambda i, j: (0, up(j))),
            pl.BlockSpec((tf, d), lambda i, j: (down(j), 0)),
            pl.BlockSpec((1, d), lambda i, j: (0, 0)),
        ],
        out_specs=pl.BlockSpec((tm, d), lambda i, j: (i, 0), pipeline_mode=pl.Buffered(1)),
        out_shape=jax.ShapeDtypeStruct((t, d), F32),
        scratch_shapes=[pltpu.VMEM((tm + 2 * HALO, d), BF16),
                        pltpu.VMEM((tm + 2 * HALO, tf), F32),
                        pltpu.VMEM((tm, tf), BF16),
                        pltpu.VMEM((tm, tf), BF16)],
        compiler_params=_params(("arbitrary", "arbitrary")),
        name="ffn",
    )(x1, x1, x1, mod, g_ffn, w_up, w_up, cw_pad, ffn_conv_b, w_down, g_final)


def kernel(x, c, w_ada, b_ada, g_mix, w_in, conv_w, conv_b, conv_ln_g, conv_ln_b, sgu_ln_g, sgu_ln_b, sgu_w, sgu_b, out_g_conv, out_g_sgu, w_out, g_ffn, w_up, ffn_conv_w, ffn_conv_b, w_down, g_final):
    bsz, seq, d = x.shape
    depth = w_ada.shape[0]
    d_conv = conv_w.shape[2]
    n_mod = w_ada.shape[2] // d
    xf = x.reshape(bsz * seq, d)
    c_pad = jnp.pad(c, ((0, (-bsz) % 8), (0, 0)))

    for l in range(depth):
        mod = _ada(c_pad, w_ada[l], b_ada[l][None, :])[:bsz].reshape(bsz, n_mod, d)
        mod = jnp.pad(mod, ((0, 0), (0, N_MOD_PAD - n_mod), (0, 0)))

        a_glu, z, wo_bf, wd_bf = _inproj(xf, mod, g_mix[l][None, :], w_in[l], w_out[l],
                                         w_down[l], d_conv, seq)
        ya, wu_bf = _convbr(a_glu, conv_w[l], conv_b[l][None, :], conv_ln_g[l][None, :],
                            conv_ln_b[l][None, :], out_g_conv[l][None, :], w_up[l], seq)
        bias_full = jnp.repeat(sgu_b[l].T, HEAD_DIM, axis=1)
        yb = _sgubr(z, sgu_w[l].astype(BF16), bias_full, sgu_ln_g[l][None, :],
                    sgu_ln_b[l][None, :], out_g_sgu[l][None, :])
        x1 = _outproj(ya, yb, wo_bf, xf, mod, seq)
        xf = _ffn(x1, mod, g_ffn[l][None, :], wu_bf, ffn_conv_w[l],
                  ffn_conv_b[l][None, :], wd_bf, g_final[None, :], seq,
                  final_norm=(l == depth - 1))

    return xf.reshape(bsz, seq, d)
```

```python
import functools
import math

import jax
import jax.numpy as jnp
from jax import lax
from jax.experimental import pallas as pl
from jax.experimental.pallas import tpu as pltpu

EPS = 1e-6
CHUNK = 128
HEAD_DIM = 128
SUBLANE = 8
LANE = 128
HALO = 16
N_MOD_PAD = 8
VMEM_LIMIT_BYTES = 62 * 1024 * 1024

F32 = jnp.float32
BF16 = jnp.bfloat16


def _dot(a, b):
    return jnp.dot(a, b, preferred_element_type=F32)


def _sigmoid(x):
    return 0.5 * (1.0 + jnp.tanh(0.5 * x))


def _silu(x):
    return x * _sigmoid(x)


def _rms_rows(x, g):
    return x * lax.rsqrt(jnp.mean(x * x, axis=-1, keepdims=True) + EPS) * g


def _modulate(x, g, scale, shift):
    return _rms_rows(x, g) * (1.0 + scale) + shift


def _cast_block(src_ref, dst_ref, lanes=512):
    def body(c, carry):
        sl = pl.ds(pl.multiple_of(c * lanes, lanes), lanes)
        dst_ref[:, sl] = src_ref[:, sl].astype(BF16)
        return carry

    lax.fori_loop(0, src_ref.shape[1] // lanes, body, 0)


def _params(sem):
    return pltpu.CompilerParams(dimension_semantics=sem, vmem_limit_bytes=VMEM_LIMIT_BYTES)


def _ada_kernel(c_ref, w_ref, b_ref, o_ref):
    ca = _silu(c_ref[...]).astype(BF16)
    o_ref[...] = _dot(ca, w_ref[...].astype(BF16)) + b_ref[...]


def _ada(c_pad, w_ada, b_ada, tn=512):
    rows, d = c_pad.shape
    n = w_ada.shape[1]
    return pl.pallas_call(
        _ada_kernel,
        grid=(n // tn,),
        in_specs=[
            pl.BlockSpec((rows, d), lambda j: (0, 0)),
            pl.BlockSpec((d, tn), lambda j: (0, j)),
            pl.BlockSpec((1, tn), lambda j: (0, j)),
        ],
        out_specs=pl.BlockSpec((rows, tn), lambda j: (0, j)),
        out_shape=jax.ShapeDtypeStruct((rows, n), F32),
        compiler_params=_params(("arbitrary",)),
        name="ada",
    )(c_pad, w_ada, b_ada)


def _inproj_kernel(x_ref, mod_ref, g_ref, wm_ref, wg_ref, wo_ref, wd_ref, a_ref, z_ref, wob_ref,
                   wdb_ref, h_scr, *, n_glu, rows):
    j = pl.program_id(1)
    _cast_block(wo_ref, wob_ref)
    _cast_block(wd_ref, wdb_ref)

    @pl.when(j == 0)
    def _():
        g = g_ref[...]
        shift = mod_ref[0, 0:1, :]
        scale = mod_ref[0, 1:2, :]

        def body(r, carry):
            sl = pl.ds(pl.multiple_of(r * rows, rows), rows)
            h_scr[sl, :] = _modulate(x_ref[sl, :], g, scale, shift).astype(BF16)
            return carry

        lax.fori_loop(0, x_ref.shape[0] // rows, body, 0)

    @pl.when(j < n_glu)
    def _():
        h = h_scr[...]
        a_ref[...] = _dot(h, wm_ref[...]) * _sigmoid(_dot(h, wg_ref[...]))

    @pl.when(j >= n_glu)
    def _():
        p = _dot(h_scr[...], wm_ref[...])
        z_ref[...] = 0.5 * p * (1.0 + lax.erf(p * math.sqrt(0.5)))


def _cast_spec(w, n_steps, step):
    rows = -(-w.shape[0] // n_steps // HALO) * HALO
    last = -(-w.shape[0] // rows) - 1
    return pl.BlockSpec((rows, w.shape[1]), lambda *ids: (jnp.minimum(step(*ids), last), 0))


def _inproj(x2d, mod, g_mix, w_in, w_out, w_down, d_conv, seq, tm=512, tn=512):
    t, d = x2d.shape
    n_glu = d_conv // tn
    n_z = (w_in.shape[1] - 2 * d_conv) // tn
    n_j = n_glu + n_z
    n_steps = (t // tm) * n_j
    wo_spec = _cast_spec(w_out, n_steps, lambda i, j: i * n_j + j)
    wd_spec = _cast_spec(w_down, n_steps, lambda i, j: i * n_j + j)
    w_bf = w_in.astype(BF16)
    tiles_per_seq = seq // tm
    kern = functools.partial(_inproj_kernel, n_glu=n_glu, rows=64)
    return pl.pallas_call(
        kern,
        grid=(t // tm, n_glu + n_z),
        in_specs=[
            pl.BlockSpec((tm, d), lambda i, j: (i, 0)),
            pl.BlockSpec((1, N_MOD_PAD, d), lambda i, j: (i // tiles_per_seq, 0, 0)),
            pl.BlockSpec((1, d), lambda i, j: (0, 0)),
            pl.BlockSpec((d, tn), lambda i, j: (0, j + jnp.where(j >= n_glu, n_glu, 0))),
            pl.BlockSpec((d, tn), lambda i, j: (0, jnp.minimum(j + n_glu, 2 * n_glu - 1))),
            wo_spec,
            wd_spec,
        ],
        out_specs=[
            pl.BlockSpec((tm, tn), lambda i, j: (i, jnp.minimum(j, n_glu - 1))),
            pl.BlockSpec((tm, tn), lambda i, j: (i, jnp.maximum(j - n_glu, 0))),
            wo_spec,
            wd_spec,
        ],
        out_shape=[
            jax.ShapeDtypeStruct((t, d_conv), F32),
            jax.ShapeDtypeStruct((t, n_z * tn), F32),
            jax.ShapeDtypeStruct(w_out.shape, BF16),
            jax.ShapeDtypeStruct(w_down.shape, BF16),
        ],
        scratch_shapes=[pltpu.VMEM((tm, d), BF16)],
        compiler_params=_params(("arbitrary", "arbitrary")),
        name="inproj",
    )(x2d, mod, g_mix, w_bf, w_bf, w_out, w_down)


def _convbr_kernel(prev_ref, main_ref, next_ref, w_ref, cb_ref, lg_ref, lb_ref, og_ref, wu_ref,
                   y_ref, wub_ref, buf, cbuf, *, ts, tiles_per_seq, taps, rb, norm_rows,
                   norm_unroll):
    i = pl.program_id(0)
    _cast_block(wu_ref, wub_ref)
    first = (i % tiles_per_seq) == 0
    last = (i % tiles_per_seq) == tiles_per_seq - 1
    buf[0:HALO, :] = jnp.where(first, 0.0, prev_ref[...])
    buf[HALO:HALO + ts, :] = main_ref[...]
    buf[HALO + ts:2 * HALO + ts, :] = jnp.where(last, 0.0, next_ref[...])
    c = main_ref.shape[1]
    off = HALO - (taps - 1) // 2
    n_lane = c // LANE
    n_out = rb // SUBLANE
    n_win = n_out + 2 * HALO // SUBLANE
    row_iota = lax.broadcasted_iota(jnp.int32, (SUBLANE, LANE), 0)

    def conv_body(t, carry):
        base = pl.multiple_of((t // n_lane) * rb, rb)
        lane = pl.ds(pl.multiple_of((t % n_lane) * LANE, LANE), LANE)
        win = [buf[pl.ds(base + SUBLANE * m, SUBLANE), lane] for m in range(n_win)]
        acc = [cb_ref[:, lane]] * n_out
        for rho in range(SUBLANE):
            if rho == 0:
                sh = win
            else:
                rolled = [pltpu.roll(w, SUBLANE - rho, 0) for w in win]
                lower = row_iota < (SUBLANE - rho)
                sh = [jnp.where(lower, rolled[m], rolled[m + 1]) for m in range(n_win - 1)]
            for q in range(2 * HALO // SUBLANE):
                k = SUBLANE * q + rho - off
                if 0 <= k < taps:
                    wv = w_ref[k, :, lane]
                    acc = [acc[m] + wv * sh[m + q] for m in range(n_out)]
        for m in range(n_out):
            cbuf[pl.ds(base + SUBLANE * m, SUBLANE), lane] = acc[m]
        return carry

    lax.fori_loop(0, (ts // rb) * n_lane, conv_body, 0)

    lg = lg_ref[...]
    lb = lb_ref[...]
    og = og_ref[...]

    def norm_body(r, carry):
        sl = pl.ds(pl.multiple_of(r * norm_rows, norm_rows), norm_rows)
        a = cbuf[sl, :]
        xc = a - jnp.mean(a, axis=-1, keepdims=True)
        var = jnp.mean(xc * xc, axis=-1, keepdims=True)
        s = _silu(xc * lax.rsqrt(var + EPS) * lg + lb)
        y_ref[sl, :] = _rms_rows(s, og).astype(BF16)
        return carry

    lax.fori_loop(0, ts // norm_rows, norm_body, 0, unroll=norm_unroll)


def _convbr(a_glu, conv_w, conv_b, ln_g, ln_b, out_g, w_up, seq, ts=256):
    t, c = a_glu.shape
    wu_rows = w_up.shape[0] // (t // ts)
    assert wu_rows * (t // ts) == w_up.shape[0] and wu_rows % HALO == 0
    taps = conv_w.shape[0]
    w_b = jnp.broadcast_to(conv_w[:, None, :], (taps, SUBLANE, c))
    cb_b = jnp.broadcast_to(conv_b, (SUBLANE, c))
    tiles_per_seq = seq // ts
    hb = ts // HALO
    n_hb = t // HALO
    kern = functools.partial(_convbr_kernel, ts=ts, tiles_per_seq=tiles_per_seq, taps=taps,
                             rb=64, norm_rows=16, norm_unroll=4)
    row = lambda i: (0, 0)
    return pl.pallas_call(
        kern,
        grid=(t // ts,),
        in_specs=[
            pl.BlockSpec((HALO, c), lambda i: (jnp.maximum(i * hb - 1, 0), 0)),
            pl.BlockSpec((ts, c), lambda i: (i, 0)),
            pl.BlockSpec((HALO, c), lambda i: (jnp.minimum((i + 1) * hb, n_hb - 1), 0)),
            pl.BlockSpec(w_b.shape, lambda i: (0, 0, 0)),
            pl.BlockSpec((SUBLANE, c), row),
            pl.BlockSpec((1, c), row),
            pl.BlockSpec((1, c), row),
            pl.BlockSpec((1, c), row),
            pl.BlockSpec((wu_rows, w_up.shape[1]), lambda i: (i, 0)),
        ],
        out_specs=[pl.BlockSpec((ts, c), lambda i: (i, 0)),
                   pl.BlockSpec((wu_rows, w_up.shape[1]), lambda i: (i, 0))],
        out_shape=[jax.ShapeDtypeStruct((t, c), BF16),
                   jax.ShapeDtypeStruct(w_up.shape, BF16)],
        scratch_shapes=[pltpu.VMEM((ts + 2 * HALO, c), F32), pltpu.VMEM((ts, c), F32)],
        compiler_params=_params(("arbitrary",)),
        name="convbr",
    )(a_glu, a_glu, a_glu, w_b, cb_b, ln_g, ln_b, out_g, w_up)


def _sgubr_kernel(u_ref, v_ref, w_ref, bias_ref, lg_ref, lb_ref, og_ref, y_ref, vn_scr, b_scr,
                  *, ts, rows):
    lg = lg_ref[...]
    lb = lb_ref[...]
    og = og_ref[...]
    n_chunks = ts // CHUNK
    n_heads = v_ref.shape[1] // HEAD_DIM

    def ln_body(r, carry):
        sl = pl.ds(pl.multiple_of(r * rows, rows), rows)
        v = v_ref[sl, :]
        xc = v - jnp.mean(v, axis=-1, keepdims=True)
        var = jnp.mean(xc * xc, axis=-1, keepdims=True)
        vn_scr[sl, :] = (xc * lax.rsqrt(var + EPS) * lg + lb).astype(BF16)
        return carry

    lax.fori_loop(0, ts // rows, ln_body, 0, unroll=4)

    for h in range(n_heads):
        hs = slice(h * HEAD_DIM, (h + 1) * HEAD_DIM)
        rhs = jnp.concatenate(
            [vn_scr[ck * CHUNK:(ck + 1) * CHUNK, hs] for ck in range(n_chunks)], axis=1)
        mixed = _dot(w_ref[h], rhs)
        bias = bias_ref[:, hs]
        for ck in range(n_chunks):
            rs = slice(ck * CHUNK, (ck + 1) * CHUNK)
            m = mixed[:, ck * HEAD_DIM:(ck + 1) * HEAD_DIM] + bias
            b_scr[rs, hs] = u_ref[rs, hs] * m

    def rms_body(r, carry):
        sl = pl.ds(pl.multiple_of(r * rows, rows), rows)
        y_ref[sl, :] = _rms_rows(b_scr[sl, :], og).astype(BF16)
        return carry

    lax.fori_loop(0, ts // rows, rms_body, 0, unroll=4)


def _sgubr(z, sgu_w, bias_full, ln_g, ln_b, out_g, ts=256):
    t, two_c = z.shape
    c = two_c // 2
    kern = functools.partial(_sgubr_kernel, ts=ts, rows=16)
    row = lambda i: (0, 0)
    return pl.pallas_call(
        kern,
        grid=(t // ts,),
        in_specs=[
            pl.BlockSpec((ts, c), lambda i: (i, 0)),
            pl.BlockSpec((ts, c), lambda i: (i, 1)),
            pl.BlockSpec(sgu_w.shape, lambda i: (0, 0, 0)),
            pl.BlockSpec(bias_full.shape, row),
            pl.BlockSpec((1, c), row),
            pl.BlockSpec((1, c), row),
            pl.BlockSpec((1, c), row),
        ],
        out_specs=pl.BlockSpec((ts, c), lambda i: (i, 0)),
        out_shape=jax.ShapeDtypeStruct((t, c), BF16),
        scratch_shapes=[pltpu.VMEM((ts, c), BF16), pltpu.VMEM((ts, c), F32)],
        compiler_params=_params(("arbitrary",)),
        name="sgubr",
    )(z, z, sgu_w, bias_full, ln_g, ln_b, out_g)


def _outproj_kernel(ya_ref, yb_ref, wa_ref, wb_ref, x_ref, mod_ref, o_ref):
    y = _dot(ya_ref[...], wa_ref[...]) + _dot(yb_ref[...], wb_ref[...])
    o_ref[...] = x_ref[...] + mod_ref[0, 2:3, :] * y


def _outproj(ya, yb, w_out, x2d, mod, seq, tm=1024, tn=1024):
    t, c = ya.shape
    d = x2d.shape[1]
    tiles_per_seq = seq // tm
    return pl.pallas_call(
        _outproj_kernel,
        grid=(t // tm, d // tn),
        in_specs=[
            pl.BlockSpec((tm, c), lambda i, j: (i, 0)),
            pl.BlockSpec((tm, c), lambda i, j: (i, 0)),
            pl.BlockSpec((c, tn), lambda i, j: (0, j)),
            pl.BlockSpec((c, tn), lambda i, j: (1, j)),
            pl.BlockSpec((tm, tn), lambda i, j: (i, j)),
            pl.BlockSpec((1, N_MOD_PAD, tn), lambda i, j: (i // tiles_per_seq, 0, j)),
        ],
        out_specs=pl.BlockSpec((tm, tn), lambda i, j: (i, j)),
        out_shape=jax.ShapeDtypeStruct((t, d), F32),
        compiler_params=_params(("arbitrary", "arbitrary")),
        name="outproj",
    )(ya, yb, w_out, w_out, x2d, mod)


def _gated_act(g_ref, v_ref, cw_ref, cb_ref, act_ref, tm):
    tf = v_ref.shape[1]
    rb = 4 * SUBLANE
    n_grp = rb // SUBLANE
    row = lax.broadcasted_iota(jnp.int32, (SUBLANE, LANE), 0)
    for lc in range(tf // LANE):
        lane = slice(lc * LANE, (lc + 1) * LANE)
        w0, w1, w2, cb = cw_ref[0, :, lane], cw_ref[1, :, lane], cw_ref[2, :, lane], cb_ref[:, lane]
        for r0 in range(0, tm, rb):
            g0 = HALO - SUBLANE + r0
            win = [g_ref[g0 + SUBLANE * m:g0 + SUBLANE * (m + 1), lane] for m in range(n_grp + 2)]
            dn = [pltpu.roll(w, 1, 0) for w in win]
            up = [pltpu.roll(w, SUBLANE - 1, 0) for w in win]
            out = []
            for m in range(n_grp):
                prev = jnp.where(row == 0, dn[m], dn[m + 1])
                nxt = jnp.where(row == SUBLANE - 1, up[m + 2], up[m + 1])
                conv = w0 * prev + w1 * win[m + 1] + w2 * nxt + cb
                out.append(_silu(conv) * v_ref[r0 + SUBLANE * m:r0 + SUBLANE * (m + 1), lane])
            act_ref[r0:r0 + rb, lane] = jnp.concatenate(out, axis=0).astype(BF16)


def _ffn_kernel(prev_ref, main_ref, next_ref, mod_ref, g_ref, wg_ref, wv_ref, cw_ref, cb_ref,
                wd_ref, gf_ref, o_ref, h_scr, g_scr, v_scr, act0_scr, act1_scr, *, tm,
                tiles_per_seq, rows, final_norm):
    act_scr = (act0_scr, act1_scr)
    i = pl.program_id(0)
    j = pl.program_id(1)

    @pl.when(j == 0)
    def _():
        g = g_ref[...]
        shift = mod_ref[0, 3:4, :]
        scale = mod_ref[0, 4:5, :]
        first = (i % tiles_per_seq) == 0
        last = (i % tiles_per_seq) == tiles_per_seq - 1
        hp = _modulate(prev_ref[...], g, scale, shift)
        hn = _modulate(next_ref[...], g, scale, shift)
        h_scr[0:HALO, :] = jnp.where(first, 0.0, hp).astype(BF16)
        h_scr[HALO + tm:2 * HALO + tm, :] = jnp.where(last, 0.0, hn).astype(BF16)

        def body(r, carry):
            src = pl.ds(pl.multiple_of(r * rows, rows), rows)
            dst = pl.ds(pl.multiple_of(HALO + r * rows, HALO), rows)
            h_scr[dst, :] = _modulate(main_ref[src, :], g, scale, shift).astype(BF16)
            o_ref[src, :] = jnp.zeros((rows, o_ref.shape[1]), F32)
            return carry

        lax.fori_loop(0, tm // rows, body, 0)

    def up_and_act(slot):
        g_scr[...] = _dot(h_scr[...], wg_ref[...])
        v_scr[...] = _dot(h_scr[HALO:HALO + tm, :], wv_ref[...])
        _gated_act(g_scr, v_scr, cw_ref, cb_ref, act_scr[slot], tm)

    def down_proj(slot):
        o_ref[...] += _dot(act_scr[slot][...], wd_ref[...])

    n_f = pl.num_programs(1) - 1
    middle = jnp.logical_and(j > 0, j < n_f)

    @pl.when(j == 0)
    def _():
        up_and_act(0)

    for parity in range(2):
        @pl.when(jnp.logical_and(middle, j % 2 == parity))
        def _():
            up_and_act(parity)
            down_proj(1 - parity)

        @pl.when(jnp.logical_and(j == n_f, j % 2 == parity))
        def _():
            down_proj(1 - parity)

    @pl.when(j == pl.num_programs(1) - 1)
    def _():
        gate = mod_ref[0, 5:6, :]
        gf = gf_ref[...]

        def body(r, carry):
            sl = pl.ds(pl.multiple_of(r * rows, rows), rows)
            x2 = main_ref[sl, :] + gate * o_ref[sl, :]
            o_ref[sl, :] = _rms_rows(x2, gf) if final_norm else x2
            return carry

        lax.fori_loop(0, tm // rows, body, 0)


def _ffn(x1, mod, g_ffn, w_up, ffn_conv_w, ffn_conv_b, w_down, g_final, seq, final_norm,
         tm=1024, tf=256):
    t, d = x1.shape
    f = w_down.shape[0]
    n_f = f // tf
    tiles_per_seq = seq // tm
    hb = tm // HALO
    n_hb = t // HALO
    cw_b = jnp.broadcast_to(ffn_conv_w[:, None, :], (ffn_conv_w.shape[0], SUBLANE, f))
    cb_b = jnp.broadcast_to(ffn_conv_b, (SUBLANE, f))
    kern = functools.partial(_ffn_kernel, tm=tm, tiles_per_seq=tiles_per_seq, rows=64,
                             final_norm=final_norm)
    up = lambda j: jnp.minimum(j, n_f - 1)
    down = lambda j: jnp.maximum(j - 1, 0)
    return pl.pallas_call(
        kern,
        grid=(t // tm, n_f + 1),
        in_specs=[
            pl.BlockSpec((HALO, d), lambda i, j: (jnp.maximum(i * hb - 1, 0), 0)),
            pl.BlockSpec((tm, d), lambda i, j: (i, 0), pipeline_mode=pl.Buffered(1)),
            pl.BlockSpec((HALO, d), lambda i, j: (jnp.minimum((i + 1) * hb, n_hb - 1), 0)),
            pl.BlockSpec((1, N_MOD_PAD, d), lambda i, j: (i // tiles_per_seq, 0, 0)),
            pl.BlockSpec((1, d), lambda i, j: (0, 0)),
            pl.BlockSpec((d, tf), lambda i, j: (0, up(j))),
            pl.BlockSpec((d, tf), lambda i, j: (0, up(j) + n_f)),
            pl.BlockSpec((cw_b.shape[0], SUBLANE, tf), lambda i, j: (0, 0, up(j))),
            pl.BlockSpec((SUBLANE, tf), lambda i, j: (0, up(j))),
            pl.BlockSpec((tf, d), lambda i, j: (down(j), 0)),
            pl.BlockSpec((1, d), lambda i, j: (0, 0)),
        ],
        out_specs=pl.BlockSpec((tm, d), lambda i, j: (i, 0), pipeline_mode=pl.Buffered(1)),
        out_shape=jax.ShapeDtypeStruct((t, d), F32),
        scratch_shapes=[pltpu.VMEM((tm + 2 * HALO, d), BF16),
                        pltpu.VMEM((tm + 2 * HALO, tf), F32),
                        pltpu.VMEM((tm, tf), F32),
                        pltpu.VMEM((tm, tf), BF16),
                        pltpu.VMEM((tm, tf), BF16)],
        compiler_params=_params(("arbitrary", "arbitrary")),
        name="ffn",
    )(x1, x1, x1, mod, g_ffn, w_up, w_up, cw_b, cb_b, w_down, g_final)


def kernel(x, c, w_ada, b_ada, g_mix, w_in, conv_w, conv_b, conv_ln_g, conv_ln_b, sgu_ln_g, sgu_ln_b, sgu_w, sgu_b, out_g_conv, out_g_sgu, w_out, g_ffn, w_up, ffn_conv_w, ffn_conv_b, w_down, g_final):
    bsz, seq, d = x.shape
    depth = w_ada.shape[0]
    d_conv = conv_w.shape[2]
    n_mod = w_ada.shape[2] // d
    xf = x.reshape(bsz * seq, d)
    c_pad = jnp.pad(c, ((0, (-bsz) % 8), (0, 0)))

    for l in range(depth):
        mod = _ada(c_pad, w_ada[l], b_ada[l][None, :])[:bsz].reshape(bsz, n_mod, d)
        mod = jnp.pad(mod, ((0, 0), (0, N_MOD_PAD - n_mod), (0, 0)))

        a_glu, z, wo_bf, wd_bf = _inproj(xf, mod, g_mix[l][None, :], w_in[l], w_out[l],
                                         w_down[l], d_conv, seq)
        ya, wu_bf = _convbr(a_glu, conv_w[l], conv_b[l][None, :], conv_ln_g[l][None, :],
                            conv_ln_b[l][None, :], out_g_conv[l][None, :], w_up[l], seq)
        bias_full = jnp.repeat(sgu_b[l].T, HEAD_DIM, axis=1)
        yb = _sgubr(z, sgu_w[l].astype(BF16), bias_full, sgu_ln_g[l][None, :],
                    sgu_ln_b[l][None, :], out_g_sgu[l][None, :])
        x1 = _outproj(ya, yb, wo_bf, xf, mod, seq)
        xf = _ffn(x1, mod, g_ffn[l][None, :], wu_bf, ffn_conv_w[l],
                  ffn_conv_b[l][None, :], wd_bf, g_final[None, :], seq,
                  final_norm=(l == depth - 1))

    return xf.reshape(bsz, seq, d)
```

```python
import functools
import math

import jax
import jax.numpy as jnp
from jax import lax
from jax.experimental import pallas as pl
from jax.experimental.pallas import tpu as pltpu

EPS = 1e-6
CHUNK = 128
HEAD_DIM = 128
SUBLANE = 8
LANE = 128
HALO = 16
N_MOD_PAD = 8
VMEM_LIMIT_BYTES = 63 * 1024 * 1024
X_SLOTS = 4
FFN_TILE = 512

F32 = jnp.float32
BF16 = jnp.bfloat16


def _dot(a, b):
    return jnp.dot(a, b, preferred_element_type=F32)


def _sigmoid(x):
    return 0.5 * (1.0 + jnp.tanh(0.5 * x))


def _silu(x):
    return x * _sigmoid(x)


def _rms_rows(x, g):
    return x * lax.rsqrt(jnp.mean(x * x, axis=-1, keepdims=True) + EPS) * g


def _modulate(x, g, scale, shift):
    return _rms_rows(x, g) * (1.0 + scale) + shift


def _cast_block(src_ref, dst_ref, lanes=512):
    def body(c, carry):
        sl = pl.ds(pl.multiple_of(c * lanes, lanes), lanes)
        dst_ref[:, sl] = src_ref[:, sl].astype(BF16)
        return carry

    lax.fori_loop(0, src_ref.shape[1] // lanes, body, 0)


def _cast_halves_padded(src_ref, dst_ref, f, f_pad, lanes=256):
    n = f // lanes
    pad = (f_pad - f) // lanes
    assert n * lanes == f and pad * lanes == f_pad - f

    def body(c, carry):
        src = pl.ds(pl.multiple_of(c * lanes, lanes), lanes)
        dst = pl.ds(pl.multiple_of((c + jnp.where(c < n, 0, pad)) * lanes, lanes), lanes)
        dst_ref[:, dst] = src_ref[:, src].astype(BF16)
        return carry

    lax.fori_loop(0, 2 * n, body, 0)
    if pad:
        zeros = jnp.zeros((dst_ref.shape[0], f_pad - f), BF16)
        dst_ref[:, f:f_pad] = zeros
        dst_ref[:, f_pad + f:2 * f_pad] = zeros


def _params(sem):
    return pltpu.CompilerParams(dimension_semantics=sem, vmem_limit_bytes=VMEM_LIMIT_BYTES)


def _ada_kernel(c_ref, w_ref, b_ref, o_ref):
    ca = _silu(c_ref[...]).astype(BF16)
    o_ref[...] = _dot(ca, w_ref[...].astype(BF16)) + b_ref[...]


def _ada(c_pad, w_ada, b_ada, tn=512):
    rows, d = c_pad.shape
    n = w_ada.shape[1]
    return pl.pallas_call(
        _ada_kernel,
        grid=(n // tn,),
        in_specs=[
            pl.BlockSpec((rows, d), lambda j: (0, 0)),
            pl.BlockSpec((d, tn), lambda j: (0, j)),
            pl.BlockSpec((1, tn), lambda j: (0, j)),
        ],
        out_specs=pl.BlockSpec((rows, tn), lambda j: (0, j)),
        out_shape=jax.ShapeDtypeStruct((rows, n), F32),
        compiler_params=_params(("arbitrary",)),
        name="ada",
    )(c_pad, w_ada, b_ada)


def _inproj_kernel(x_ref, mod_ref, g_ref, wm_ref, wg_ref, wo_ref, wd_ref, a_ref, z_ref, wob_ref,
                   wdb_ref, h_scr, *, n_glu, rows, wd_blocks):
    j = pl.program_id(1)
    _cast_block(wo_ref, wob_ref)

    step = pl.program_id(0) * pl.num_programs(1) + j

    @pl.when(step < wd_blocks)
    def _():
        _cast_block(wd_ref, wdb_ref)

    @pl.when(step >= wd_blocks)
    def _():
        wdb_ref[...] = jnp.zeros(wdb_ref.shape, BF16)


    @pl.when(j == 0)
    def _():
        g = g_ref[...]
        shift = mod_ref[0, 0:1, :]
        scale = mod_ref[0, 1:2, :]

        def body(r, carry):
            sl = pl.ds(pl.multiple_of(r * rows, rows), rows)
            h_scr[sl, :] = _modulate(x_ref[sl, :], g, scale, shift).astype(BF16)
            return carry

        lax.fori_loop(0, x_ref.shape[0] // rows, body, 0, unroll=4)

    @pl.when(j < n_glu)
    def _():
        h = h_scr[...]
        a_ref[...] = _dot(h, wm_ref[...]) * _sigmoid(_dot(h, wg_ref[...]))

    @pl.when(j >= n_glu)
    def _():
        p = _dot(h_scr[...], wm_ref[...])
        z_ref[...] = 0.5 * p * (1.0 + lax.erf(p * math.sqrt(0.5)))


def _cast_spec(w, n_steps, step):
    rows = -(-w.shape[0] // n_steps // HALO) * HALO
    last = -(-w.shape[0] // rows) - 1
    return pl.BlockSpec((rows, w.shape[1]), lambda *ids: (jnp.minimum(step(*ids), last), 0))


def _inproj(x2d, mod, g_mix, w_in, w_out, w_down, f_pad, d_conv, seq, tm=512, tn=512):
    t, d = x2d.shape
    n_glu = d_conv // tn
    n_z = (w_in.shape[1] - 2 * d_conv) // tn
    n_j = n_glu + n_z
    n_steps = (t // tm) * n_j
    flat = lambda i, j: i * n_j + j
    wo_spec = _cast_spec(w_out, n_steps, flat)
    f = w_down.shape[0]
    wd_rows = -(-f_pad // n_steps // HALO) * HALO
    wd_blocks = f // wd_rows
    assert wd_blocks * wd_rows == f and f_pad % wd_rows == 0 and f_pad // wd_rows <= n_steps
    wd_in_spec = pl.BlockSpec((wd_rows, d), lambda i, j: (jnp.minimum(flat(i, j), wd_blocks - 1), 0))
    wd_out_spec = pl.BlockSpec((wd_rows, d),
                               lambda i, j: (jnp.minimum(flat(i, j), f_pad // wd_rows - 1), 0))
    w_bf = w_in.astype(BF16)
    tiles_per_seq = seq // tm
    kern = functools.partial(_inproj_kernel, n_glu=n_glu, rows=16, wd_blocks=wd_blocks)
    return pl.pallas_call(
        kern,
        grid=(t // tm, n_glu + n_z),
        in_specs=[
            pl.BlockSpec((tm, d), lambda i, j: (i, 0)),
            pl.BlockSpec((1, N_MOD_PAD, d), lambda i, j: (i // tiles_per_seq, 0, 0)),
            pl.BlockSpec((1, d), lambda i, j: (0, 0)),
            pl.BlockSpec((d, tn), lambda i, j: (0, j + jnp.where(j >= n_glu, n_glu, 0))),
            pl.BlockSpec((d, tn), lambda i, j: (0, jnp.minimum(j + n_glu, 2 * n_glu - 1))),
            wo_spec,
            wd_in_spec,
        ],
        out_specs=[
            pl.BlockSpec((tm, tn), lambda i, j: (i, jnp.minimum(j, n_glu - 1))),
            pl.BlockSpec((tm, tn), lambda i, j: (i, jnp.maximum(j - n_glu, 0))),
            wo_spec,
            wd_out_spec,
        ],
        out_shape=[
            jax.ShapeDtypeStruct((t, d_conv), F32),
            jax.ShapeDtypeStruct((t, n_z * tn), F32),
            jax.ShapeDtypeStruct(w_out.shape, BF16),
            jax.ShapeDtypeStruct((f_pad, d), BF16),
        ],
        scratch_shapes=[pltpu.VMEM((tm, d), BF16)],
        compiler_params=_params(("arbitrary", "arbitrary")),
        name="inproj",
    )(x2d, mod, g_mix, w_bf, w_bf, w_out, w_down)


def _convbr_kernel(prev_ref, main_ref, next_ref, w_ref, cb_ref, lg_ref, lb_ref, og_ref, wu_ref,
                   y_ref, wub_ref, buf, cbuf, *, ts, tiles_per_seq, taps, rb, norm_rows,
                   norm_unroll):
    i = pl.program_id(0)
    _cast_halves_padded(wu_ref, wub_ref, wu_ref.shape[1] // 2, wub_ref.shape[1] // 2)
    first = (i % tiles_per_seq) == 0
    last = (i % tiles_per_seq) == tiles_per_seq - 1
    buf[0:HALO, :] = jnp.where(first, 0.0, prev_ref[...])
    buf[HALO:HALO + ts, :] = main_ref[...]
    buf[HALO + ts:2 * HALO + ts, :] = jnp.where(last, 0.0, next_ref[...])
    c = main_ref.shape[1]
    off = HALO - (taps - 1) // 2
    n_lane = c // LANE
    n_out = rb // SUBLANE
    n_win = n_out + 2 * HALO // SUBLANE
    row_iota = lax.broadcasted_iota(jnp.int32, (SUBLANE, LANE), 0)

    def conv_body(t, carry):
        base = pl.multiple_of((t // n_lane) * rb, rb)
        lane = pl.ds(pl.multiple_of((t % n_lane) * LANE, LANE), LANE)
        win = [buf[pl.ds(base + SUBLANE * m, SUBLANE), lane] for m in range(n_win)]
        acc = [cb_ref[:, lane]] * n_out
        for rho in range(SUBLANE):
            if rho == 0:
                sh = win
            else:
                rolled = [pltpu.roll(w, SUBLANE - rho, 0) for w in win]
                lower = row_iota < (SUBLANE - rho)
                sh = [jnp.where(lower, rolled[m], rolled[m + 1]) for m in range(n_win - 1)]
            for q in range(2 * HALO // SUBLANE):
                k = SUBLANE * q + rho - off
                if 0 <= k < taps:
                    wv = w_ref[k, :, lane]
                    acc = [acc[m] + wv * sh[m + q] for m in range(n_out)]
        for m in range(n_out):
            cbuf[pl.ds(base + SUBLANE * m, SUBLANE), lane] = acc[m]
        return carry

    lax.fori_loop(0, (ts // rb) * n_lane, conv_body, 0)

    lg = lg_ref[...]
    lb = lb_ref[...]
    og = og_ref[...]

    def norm_body(r, carry):
        sl = pl.ds(pl.multiple_of(r * norm_rows, norm_rows), norm_rows)
        a = cbuf[sl, :]
        xc = a - jnp.mean(a, axis=-1, keepdims=True)
        var = jnp.mean(xc * xc, axis=-1, keepdims=True)
        s = _silu(xc * lax.rsqrt(var + EPS) * lg + lb)
        y_ref[sl, :] = _rms_rows(s, og).astype(BF16)
        return carry

    lax.fori_loop(0, ts // norm_rows, norm_body, 0, unroll=norm_unroll)


def _convbr(a_glu, conv_w, conv_b, ln_g, ln_b, out_g, w_up, f_pad, seq, ts=256):
    t, c = a_glu.shape
    wu_rows = w_up.shape[0] // (t // ts)
    assert wu_rows * (t // ts) == w_up.shape[0] and wu_rows % HALO == 0
    taps = conv_w.shape[0]
    w_b = jnp.broadcast_to(conv_w[:, None, :], (taps, SUBLANE, c))
    cb_b = jnp.broadcast_to(conv_b, (SUBLANE, c))
    tiles_per_seq = seq // ts
    hb = ts // HALO
    n_hb = t // HALO
    kern = functools.partial(_convbr_kernel, ts=ts, tiles_per_seq=tiles_per_seq, taps=taps,
                             rb=64, norm_rows=16, norm_unroll=4)
    row = lambda i: (0, 0)
    return pl.pallas_call(
        kern,
        grid=(t // ts,),
        in_specs=[
            pl.BlockSpec((HALO, c), lambda i: (jnp.maximum(i * hb - 1, 0), 0)),
            pl.BlockSpec((ts, c), lambda i: (i, 0)),
            pl.BlockSpec((HALO, c), lambda i: (jnp.minimum((i + 1) * hb, n_hb - 1), 0)),
            pl.BlockSpec(w_b.shape, lambda i: (0, 0, 0)),
            pl.BlockSpec((SUBLANE, c), row),
            pl.BlockSpec((1, c), row),
            pl.BlockSpec((1, c), row),
            pl.BlockSpec((1, c), row),
            pl.BlockSpec((wu_rows, w_up.shape[1]), lambda i: (i, 0)),
        ],
        out_specs=[pl.BlockSpec((ts, c), lambda i: (i, 0)),
                   pl.BlockSpec((wu_rows, 2 * f_pad), lambda i: (i, 0))],
        out_shape=[jax.ShapeDtypeStruct((t, c), BF16),
                   jax.ShapeDtypeStruct((w_up.shape[0], 2 * f_pad), BF16)],
        scratch_shapes=[pltpu.VMEM((ts + 2 * HALO, c), F32), pltpu.VMEM((ts, c), F32)],
        compiler_params=_params(("arbitrary",)),
        name="convbr",
    )(a_glu, a_glu, a_glu, w_b, cb_b, ln_g, ln_b, out_g, w_up)


def _sgubr_kernel(u_ref, v_ref, w_ref, bias_ref, lg_ref, lb_ref, og_ref, y_ref, vn_scr, b_scr,
                  *, ts, rows):
    lg = lg_ref[...]
    lb = lb_ref[...]
    og = og_ref[...]
    n_chunks = ts // CHUNK
    n_heads = v_ref.shape[1] // HEAD_DIM

    def ln_body(r, carry):
        sl = pl.ds(pl.multiple_of(r * rows, rows), rows)
        v = v_ref[sl, :]
        xc = v - jnp.mean(v, axis=-1, keepdims=True)
        var = jnp.mean(xc * xc, axis=-1, keepdims=True)
        vn_scr[sl, :] = (xc * lax.rsqrt(var + EPS) * lg + lb).astype(BF16)
        return carry

    lax.fori_loop(0, ts // rows, ln_body, 0, unroll=4)

    for h in range(n_heads):
        hs = slice(h * HEAD_DIM, (h + 1) * HEAD_DIM)
        rhs = jnp.concatenate(
            [vn_scr[ck * CHUNK:(ck + 1) * CHUNK, hs] for ck in range(n_chunks)], axis=1)
        mixed = _dot(w_ref[h], rhs)
        bias = bias_ref[:, hs]
        for ck in range(n_chunks):
            rs = slice(ck * CHUNK, (ck + 1) * CHUNK)
            m = mixed[:, ck * HEAD_DIM:(ck + 1) * HEAD_DIM] + bias
            b_scr[rs, hs] = u_ref[rs, hs] * m

    def rms_body(r, carry):
        sl = pl.ds(pl.multiple_of(r * rows, rows), rows)
        y_ref[sl, :] = _rms_rows(b_scr[sl, :], og).astype(BF16)
        return carry

    lax.fori_loop(0, ts // rows, rms_body, 0, unroll=4)


def _sgubr(z, sgu_w, bias_full, ln_g, ln_b, out_g, ts=256):
    t, two_c = z.shape
    c = two_c // 2
    kern = functools.partial(_sgubr_kernel, ts=ts, rows=16)
    row = lambda i: (0, 0)
    return pl.pallas_call(
        kern,
        grid=(t // ts,),
        in_specs=[
            pl.BlockSpec((ts, c), lambda i: (i, 0)),
            pl.BlockSpec((ts, c), lambda i: (i, 1)),
            pl.BlockSpec(sgu_w.shape, lambda i: (0, 0, 0)),
            pl.BlockSpec(bias_full.shape, row),
            pl.BlockSpec((1, c), row),
            pl.BlockSpec((1, c), row),
            pl.BlockSpec((1, c), row),
        ],
        out_specs=pl.BlockSpec((ts, c), lambda i: (i, 0)),
        out_shape=jax.ShapeDtypeStruct((t, c), BF16),
        scratch_shapes=[pltpu.VMEM((ts, c), BF16), pltpu.VMEM((ts, c), F32)],
        compiler_params=_params(("arbitrary",)),
        name="sgubr",
    )(z, z, sgu_w, bias_full, ln_g, ln_b, out_g)


def _outproj_kernel(ya_ref, yb_ref, wa_ref, wb_ref, x_ref, mod_ref, o_ref):
    y = _dot(ya_ref[...], wa_ref[...]) + _dot(yb_ref[...], wb_ref[...])
    o_ref[...] = x_ref[...] + mod_ref[0, 2:3, :] * y


def _outproj(ya, yb, w_out, x2d, mod, seq, tm=1024, tn=1024):
    t, c = ya.shape
    d = x2d.shape[1]
    tiles_per_seq = seq // tm
    return pl.pallas_call(
        _outproj_kernel,
        grid=(t // tm, d // tn),
        in_specs=[
            pl.BlockSpec((tm, c), lambda i, j: (i, 0)),
            pl.BlockSpec((tm, c), lambda i, j: (i, 0)),
            pl.BlockSpec((c, tn), lambda i, j: (0, j)),
            pl.BlockSpec((c, tn), lambda i, j: (1, j)),
            pl.BlockSpec((tm, tn), lambda i, j: (i, j)),
            pl.BlockSpec((1, N_MOD_PAD, tn), lambda i, j: (i // tiles_per_seq, 0, j)),
        ],
        out_specs=pl.BlockSpec((tm, tn), lambda i, j: (i, j)),
        out_shape=jax.ShapeDtypeStruct((t, d), F32),
        compiler_params=_params(("arbitrary", "arbitrary")),
        name="outproj",
    )(ya, yb, w_out, w_out, x2d, mod)


def _gated_act(g_ref, v_ref, cp_ref, act_ref, cols, tm):
    rb = 4 * SUBLANE
    n_grp = rb // SUBLANE
    row = lax.broadcasted_iota(jnp.int32, (SUBLANE, LANE), 0)
    for lc in range(cols // LANE):
        lane = slice(lc * LANE, (lc + 1) * LANE)
        w0, w1, w2, cb = (cp_ref[k, :, lane] for k in range(4))
        for r0 in range(0, tm, rb):
            g0 = HALO - SUBLANE + r0
            win = [g_ref[g0 + SUBLANE * m:g0 + SUBLANE * (m + 1), lane] for m in range(n_grp + 2)]
            dn = [pltpu.roll(w, 1, 0) for w in win]
            up = [pltpu.roll(w, SUBLANE - 1, 0) for w in win]
            out = []
            for m in range(n_grp):
                prev = jnp.where(row == 0, dn[m], dn[m + 1])
                nxt = jnp.where(row == SUBLANE - 1, up[m + 2], up[m + 1])
                conv = w0 * prev + w1 * win[m + 1] + w2 * nxt + cb
                out.append(_silu(conv) * v_ref[r0 + SUBLANE * m:r0 + SUBLANE * (m + 1), lane])
            act_ref[r0:r0 + rb, lane] = jnp.concatenate(out, axis=0).astype(BF16)


def _ffn_kernel(prev_ref, x1_hbm, next_ref, mod_ref, g_ref, wg_ref, wv_ref, cp_ref, wd_ref,
                gf_ref, o_ref, h_scr, g_scr, v_scr, act0_scr, act1_scr, xbuf, xsem, *, tm,
                tiles_per_seq, n_f, last_cols, rows, final_norm):
    act_scr = (act0_scr, act1_scr)
    i = pl.program_id(0)
    j = pl.program_id(1)
    n_chunks = tm // rows

    def x_copy(r, slot):
        row0 = pl.multiple_of(i * tm + r * rows, rows)
        return pltpu.make_async_copy(x1_hbm.at[pl.ds(row0, rows), :], xbuf.at[slot],
                                     xsem.at[slot])

    def stream_x(fn):
        n_slots = xbuf.shape[0]
        ahead = n_slots - 1
        for r in range(ahead):
            x_copy(r, r).start()

        def body(p, carry):
            for slot in range(n_slots):
                r = n_slots * p + slot
                x_copy(r, slot).wait()

                @pl.when(r + ahead < n_chunks)
                def _():
                    x_copy(r + ahead, (slot + ahead) % n_slots).start()

                fn(r, xbuf.at[slot])
            return carry

        lax.fori_loop(0, n_chunks // n_slots, body, 0)

    @pl.when(j == 0)
    def _():
        g = g_ref[...]
        shift = mod_ref[0, 3:4, :]
        scale = mod_ref[0, 4:5, :]
        first = (i % tiles_per_seq) == 0
        last = (i % tiles_per_seq) == tiles_per_seq - 1
        hp = _modulate(prev_ref[...], g, scale, shift)
        hn = _modulate(next_ref[...], g, scale, shift)
        h_scr[0:HALO, :] = jnp.where(first, 0.0, hp).astype(BF16)
        h_scr[HALO + tm:2 * HALO + tm, :] = jnp.where(last, 0.0, hn).astype(BF16)

        def modulate_chunk(r, x_ref):
            src = pl.ds(pl.multiple_of(r * rows, rows), rows)
            dst = pl.ds(pl.multiple_of(HALO + r * rows, HALO), rows)
            h_scr[dst, :] = _modulate(x_ref[...], g, scale, shift).astype(BF16)
            o_ref[src, :] = jnp.zeros((rows, o_ref.shape[1]), F32)

        stream_x(modulate_chunk)

    tf = v_scr.shape[1]

    def up_and_act(slot, cols):
        g_scr[:, :cols] = _dot(h_scr[...], wg_ref[:, :cols])
        v_scr[:, :cols] = _dot(h_scr[HALO:HALO + tm, :], wv_ref[:, :cols])
        _gated_act(g_scr, v_scr, cp_ref, act_scr[slot], cols, tm)

    def down_proj(slot, cols):
        o_ref[...] += _dot(act_scr[slot][:, :cols], wd_ref[:cols, :])

    last_slot = (n_f - 1) % 2
    assert n_f >= 3

    @pl.when(j == 0)
    def _():
        up_and_act(0, tf)

    middle = jnp.logical_and(j > 0, j < n_f - 1)
    for parity in range(2):
        @pl.when(jnp.logical_and(middle, j % 2 == parity))
        def _():
            up_and_act(parity, tf)
            down_proj(1 - parity, tf)

    @pl.when(j == n_f - 1)
    def _():
        up_and_act(last_slot, last_cols)
        down_proj(1 - last_slot, tf)

    @pl.when(j == n_f)
    def _():
        down_proj(last_slot, last_cols)

    @pl.when(j == pl.num_programs(1) - 1)
    def _():
        gate = mod_ref[0, 5:6, :]
        gf = gf_ref[...]

        def residual_chunk(r, x_ref):
            sl = pl.ds(pl.multiple_of(r * rows, rows), rows)
            x2 = x_ref[...] + gate * o_ref[sl, :]
            o_ref[sl, :] = _rms_rows(x2, gf) if final_norm else x2

        stream_x(residual_chunk)


def _ffn(x1, mod, g_ffn, w_up, ffn_conv_w, ffn_conv_b, w_down, g_final, f, seq, final_norm,
         tm=1024, tf=FFN_TILE):
    t, d = x1.shape
    f_pad = w_down.shape[0]
    n_f = f_pad // tf
    tiles_per_seq = seq // tm
    hb = tm // HALO
    n_hb = t // HALO
    cp = jnp.concatenate([ffn_conv_w, ffn_conv_b], axis=0)
    cp = jnp.pad(cp, ((0, 0), (0, f_pad - cp.shape[1])))
    cp = jnp.broadcast_to(cp[:, None, :], (cp.shape[0], SUBLANE, f_pad))
    rows = 64
    once = pl.Buffered(1)
    last_cols = f - (n_f - 1) * tf
    assert 0 < last_cols <= tf and last_cols % LANE == 0
    kern = functools.partial(_ffn_kernel, tm=tm, tiles_per_seq=tiles_per_seq, n_f=n_f,
                             last_cols=last_cols, rows=rows, final_norm=final_norm)
    up = lambda j: jnp.minimum(j, n_f - 1)
    down = lambda j: jnp.maximum(j - 1, 0)
    return pl.pallas_call(
        kern,
        grid=(t // tm, n_f + 1),
        in_specs=[
            pl.BlockSpec((HALO, d), lambda i, j: (jnp.maximum(i * hb - 1, 0), 0), pipeline_mode=once),
            pl.BlockSpec(memory_space=pl.ANY),
            pl.BlockSpec((HALO, d), lambda i, j: (jnp.minimum((i + 1) * hb, n_hb - 1), 0),
                         pipeline_mode=once),
            pl.BlockSpec((1, N_MOD_PAD, d), lambda i, j: (i // tiles_per_seq, 0, 0),
                         pipeline_mode=once),
            pl.BlockSpec((1, d), lambda i, j: (0, 0), pipeline_mode=once),
            pl.BlockSpec((d, tf), lambda i, j: (0, up(j))),
            pl.BlockSpec((d, tf), lambda i, j: (0, up(j) + n_f)),
            pl.BlockSpec((cp.shape[0], SUBLANE, tf), lambda i, j: (0, 0, up(j))),
            pl.BlockSpec((tf, d), lambda i, j: (down(j), 0)),
            pl.BlockSpec((1, d), lambda i, j: (0, 0), pipeline_mode=once),
        ],
        out_specs=pl.BlockSpec((tm, d), lambda i, j: (i, 0), pipeline_mode=once),
        out_shape=jax.ShapeDtypeStruct((t, d), F32),
        scratch_shapes=[pltpu.VMEM((tm + 2 * HALO, d), BF16),
                        pltpu.VMEM((tm + 2 * HALO, tf), F32),
                        pltpu.VMEM((tm, tf), F32),
                        pltpu.VMEM((tm, tf), BF16),
                        pltpu.VMEM((tm, tf), BF16),
                        pltpu.VMEM((X_SLOTS, rows, d), F32),
                        pltpu.SemaphoreType.DMA((X_SLOTS,))],
        compiler_params=_params(("arbitrary", "arbitrary")),
        name="ffn",
    )(x1, x1, x1, mod, g_ffn, w_up, w_up, cp, w_down, g_final)


def kernel(x, c, w_ada, b_ada, g_mix, w_in, conv_w, conv_b, conv_ln_g, conv_ln_b, sgu_ln_g, sgu_ln_b, sgu_w, sgu_b, out_g_conv, out_g_sgu, w_out, g_ffn, w_up, ffn_conv_w, ffn_conv_b, w_down, g_final):
    bsz, seq, d = x.shape
    depth = w_ada.shape[0]
    d_conv = conv_w.shape[2]
    n_mod = w_ada.shape[2] // d
    xf = x.reshape(bsz * seq, d)
    c_pad = jnp.pad(c, ((0, (-bsz) % 8), (0, 0)))

    for l in range(depth):
        mod = _ada(c_pad, w_ada[l], b_ada[l][None, :])[:bsz].reshape(bsz, n_mod, d)
        mod = jnp.pad(mod, ((0, 0), (0, N_MOD_PAD - n_mod), (0, 0)))

        f_pad = -(-w_down.shape[1] // FFN_TILE) * FFN_TILE
        a_glu, z, wo_bf, wd_bf = _inproj(xf, mod, g_mix[l][None, :], w_in[l], w_out[l],
                                         w_down[l], f_pad, d_conv, seq)
        ya, wu_bf = _convbr(a_glu, conv_w[l], conv_b[l][None, :], conv_ln_g[l][None, :],
                            conv_ln_b[l][None, :], out_g_conv[l][None, :], w_up[l], f_pad,
                            seq)
        bias_full = jnp.repeat(sgu_b[l].T, HEAD_DIM, axis=1)
        yb = _sgubr(z, sgu_w[l].astype(BF16), bias_full, sgu_ln_g[l][None, :],
                    sgu_ln_b[l][None, :], out_g_sgu[l][None, :])
        x1 = _outproj(ya, yb, wo_bf, xf, mod, seq)
        xf = _ffn(x1, mod, g_ffn[l][None, :], wu_bf, ffn_conv_w[l],
                  ffn_conv_b[l][None, :], wd_bf, g_final[None, :], w_down.shape[1], seq,
                  final_norm=(l == depth - 1))

    return xf.reshape(bsz, seq, d)
```

```python
import functools
import math

import jax
import jax.numpy as jnp
from jax import lax
from jax.experimental import pallas as pl
from jax.experimental.pallas import tpu as pltpu

EPS = 1e-6
CHUNK = 128
HEAD_DIM = 128
SUBLANE = 8
LANE = 128
HALO = 16
N_MOD_PAD = 8
VMEM_LIMIT_BYTES = 63 * 1024 * 1024
X_SLOTS = 4
X_ROWS = 64
FFN_TILE = 512

F32 = jnp.float32
BF16 = jnp.bfloat16


def _dot(a, b):
    return jnp.dot(a, b, preferred_element_type=F32)


def _sigmoid(x):
    return 0.5 * (1.0 + jnp.tanh(0.5 * x))


def _silu(x):
    return x * _sigmoid(x)


def _rms_rows(x, g):
    return x * lax.rsqrt(jnp.mean(x * x, axis=-1, keepdims=True) + EPS) * g


def _modulate(x, g, scale, shift):
    return _rms_rows(x, g) * (1.0 + scale) + shift


def _cast_block(src_ref, dst_ref, lanes=512):
    def body(c, carry):
        sl = pl.ds(pl.multiple_of(c * lanes, lanes), lanes)
        dst_ref[:, sl] = src_ref[:, sl].astype(BF16)
        return carry

    lax.fori_loop(0, src_ref.shape[1] // lanes, body, 0)


def _cast_halves_padded(src_ref, dst_ref, f, f_pad, lanes=256):
    n = f // lanes
    pad = (f_pad - f) // lanes
    assert n * lanes == f and pad * lanes == f_pad - f

    def body(c, carry):
        src = pl.ds(pl.multiple_of(c * lanes, lanes), lanes)
        dst = pl.ds(pl.multiple_of((c + jnp.where(c < n, 0, pad)) * lanes, lanes), lanes)
        dst_ref[:, dst] = src_ref[:, src].astype(BF16)
        return carry

    lax.fori_loop(0, 2 * n, body, 0)
    if pad:
        zeros = jnp.zeros((dst_ref.shape[0], f_pad - f), BF16)
        dst_ref[:, f:f_pad] = zeros
        dst_ref[:, f_pad + f:2 * f_pad] = zeros


def _stream_rows(src_hbm, row0, n_rows, buf, sem, fn):
    n_slots, rows = buf.shape[0], buf.shape[1]
    n_chunks = n_rows // rows
    ahead = n_slots - 1
    assert n_chunks * rows == n_rows and n_chunks % n_slots == 0

    def copy(r, slot):
        start = pl.multiple_of(row0 + r * rows, rows)
        return pltpu.make_async_copy(src_hbm.at[pl.ds(start, rows), :], buf.at[slot], sem.at[slot])

    for r in range(ahead):
        copy(r, r).start()

    def body(p, carry):
        for slot in range(n_slots):
            r = n_slots * p + slot
            copy(r, slot).wait()

            @pl.when(r + ahead < n_chunks)
            def _():
                copy(r + ahead, (slot + ahead) % n_slots).start()

            fn(r, buf.at[slot])
        return carry

    lax.fori_loop(0, n_chunks // n_slots, body, 0)


def _params(sem):
    return pltpu.CompilerParams(dimension_semantics=sem, vmem_limit_bytes=VMEM_LIMIT_BYTES)


def _ada_kernel(c_ref, w_ref, b_ref, o_ref):
    ca = _silu(c_ref[...]).astype(BF16)
    o_ref[...] = _dot(ca, w_ref[...].astype(BF16)) + b_ref[...]


def _ada(c_pad, w_ada, b_ada, tn=512):
    rows, d = c_pad.shape
    n = w_ada.shape[1]
    return pl.pallas_call(
        _ada_kernel,
        grid=(n // tn,),
        in_specs=[
            pl.BlockSpec((rows, d), lambda j: (0, 0)),
            pl.BlockSpec((d, tn), lambda j: (0, j)),
            pl.BlockSpec((1, tn), lambda j: (0, j)),
        ],
        out_specs=pl.BlockSpec((rows, tn), lambda j: (0, j)),
        out_shape=jax.ShapeDtypeStruct((rows, n), F32),
        compiler_params=_params(("arbitrary",)),
        name="ada",
    )(c_pad, w_ada, b_ada)


def _inproj_kernel(x_hbm, mod_ref, g_ref, wm_ref, wg_ref, wo_ref, wd_ref, a_ref, z_ref, wob_ref,
                   wdb_ref, h_scr, xbuf, xsem, *, n_glu, wd_blocks):
    i = pl.program_id(0)
    j = pl.program_id(1)
    tm = h_scr.shape[0]
    rows = xbuf.shape[1]
    _cast_block(wo_ref, wob_ref)

    step = pl.program_id(0) * pl.num_programs(1) + j

    @pl.when(step < wd_blocks)
    def _():
        _cast_block(wd_ref, wdb_ref)

    @pl.when(step >= wd_blocks)
    def _():
        wdb_ref[...] = jnp.zeros(wdb_ref.shape, BF16)


    @pl.when(j == 0)
    def _():
        g = g_ref[...]
        shift = mod_ref[0, 0:1, :]
        scale = mod_ref[0, 1:2, :]

        def modulate_chunk(r, x_ref):
            sl = pl.ds(pl.multiple_of(r * rows, rows), rows)
            h_scr[sl, :] = _modulate(x_ref[...], g, scale, shift).astype(BF16)

        _stream_rows(x_hbm, i * tm, tm, xbuf, xsem, modulate_chunk)

    @pl.when(j < n_glu)
    def _():
        h = h_scr[...]
        a_ref[...] = _dot(h, wm_ref[...]) * _sigmoid(_dot(h, wg_ref[...]))

    @pl.when(j >= n_glu)
    def _():
        p = _dot(h_scr[...], wm_ref[...])
        z_ref[...] = 0.5 * p * (1.0 + lax.erf(p * math.sqrt(0.5)))


def _cast_spec(w, n_steps, step):
    rows = -(-w.shape[0] // n_steps // HALO) * HALO
    last = -(-w.shape[0] // rows) - 1
    return pl.BlockSpec((rows, w.shape[1]), lambda *ids: (jnp.minimum(step(*ids), last), 0))


def _inproj(x2d, mod, g_mix, w_in, w_out, w_down, f_pad, d_conv, seq, tm=1024, tn=512):
    t, d = x2d.shape
    n_glu = d_conv // tn
    n_z = (w_in.shape[1] - 2 * d_conv) // tn
    n_j = n_glu + n_z
    n_steps = (t // tm) * n_j
    flat = lambda i, j: i * n_j + j
    wo_spec = _cast_spec(w_out, n_steps, flat)
    f = w_down.shape[0]
    wd_rows = -(-f_pad // n_steps // HALO) * HALO
    wd_blocks = f // wd_rows
    assert wd_blocks * wd_rows == f and f_pad % wd_rows == 0 and f_pad // wd_rows <= n_steps
    wd_in_spec = pl.BlockSpec((wd_rows, d), lambda i, j: (jnp.minimum(flat(i, j), wd_blocks - 1), 0))
    wd_out_spec = pl.BlockSpec((wd_rows, d),
                               lambda i, j: (jnp.minimum(flat(i, j), f_pad // wd_rows - 1), 0))
    w_bf = w_in.astype(BF16)
    tiles_per_seq = seq // tm
    kern = functools.partial(_inproj_kernel, n_glu=n_glu, wd_blocks=wd_blocks)
    return pl.pallas_call(
        kern,
        grid=(t // tm, n_glu + n_z),
        in_specs=[
            pl.BlockSpec(memory_space=pl.ANY),
            pl.BlockSpec((1, N_MOD_PAD, d), lambda i, j: (i // tiles_per_seq, 0, 0)),
            pl.BlockSpec((1, d), lambda i, j: (0, 0)),
            pl.BlockSpec((d, tn), lambda i, j: (0, j + jnp.where(j >= n_glu, n_glu, 0))),
            pl.BlockSpec((d, tn), lambda i, j: (0, jnp.minimum(j + n_glu, 2 * n_glu - 1))),
            wo_spec,
            wd_in_spec,
        ],
        out_specs=[
            pl.BlockSpec((tm, tn), lambda i, j: (i, jnp.minimum(j, n_glu - 1))),
            pl.BlockSpec((tm, tn), lambda i, j: (i, jnp.maximum(j - n_glu, 0))),
            wo_spec,
            wd_out_spec,
        ],
        out_shape=[
            jax.ShapeDtypeStruct((t, d_conv), F32),
            jax.ShapeDtypeStruct((t, n_z * tn), F32),
            jax.ShapeDtypeStruct(w_out.shape, BF16),
            jax.ShapeDtypeStruct((f_pad, d), BF16),
        ],
        scratch_shapes=[pltpu.VMEM((tm, d), BF16),
                        pltpu.VMEM((X_SLOTS, X_ROWS, d), F32),
                        pltpu.SemaphoreType.DMA((X_SLOTS,))],
        compiler_params=_params(("arbitrary", "arbitrary")),
        name="inproj",
    )(x2d, mod, g_mix, w_bf, w_bf, w_out, w_down)


def _convbr_kernel(prev_ref, main_ref, next_ref, w_ref, cb_ref, lg_ref, lb_ref, og_ref, wu_ref,
                   y_ref, wub_ref, buf, cbuf, *, ts, tiles_per_seq, taps, rb, norm_rows,
                   norm_unroll):
    i = pl.program_id(0)
    _cast_halves_padded(wu_ref, wub_ref, wu_ref.shape[1] // 2, wub_ref.shape[1] // 2)
    first = (i % tiles_per_seq) == 0
    last = (i % tiles_per_seq) == tiles_per_seq - 1
    buf[0:HALO, :] = jnp.where(first, 0.0, prev_ref[...])
    buf[HALO:HALO + ts, :] = main_ref[...]
    buf[HALO + ts:2 * HALO + ts, :] = jnp.where(last, 0.0, next_ref[...])
    c = main_ref.shape[1]
    off = HALO - (taps - 1) // 2
    n_lane = c // LANE
    n_out = rb // SUBLANE
    n_win = n_out + 2 * HALO // SUBLANE
    row_iota = lax.broadcasted_iota(jnp.int32, (SUBLANE, LANE), 0)

    def conv_body(t, carry):
        base = pl.multiple_of((t // n_lane) * rb, rb)
        lane = pl.ds(pl.multiple_of((t % n_lane) * LANE, LANE), LANE)
        win = [buf[pl.ds(base + SUBLANE * m, SUBLANE), lane] for m in range(n_win)]
        acc = [cb_ref[:, lane]] * n_out
        for rho in range(SUBLANE):
            if rho == 0:
                sh = win
            else:
                rolled = [pltpu.roll(w, SUBLANE - rho, 0) for w in win]
                lower = row_iota < (SUBLANE - rho)
                sh = [jnp.where(lower, rolled[m], rolled[m + 1]) for m in range(n_win - 1)]
            for q in range(2 * HALO // SUBLANE):
                k = SUBLANE * q + rho - off
                if 0 <= k < taps:
                    wv = w_ref[k, :, lane]
                    acc = [acc[m] + wv * sh[m + q] for m in range(n_out)]
        for m in range(n_out):
            cbuf[pl.ds(base + SUBLANE * m, SUBLANE), lane] = acc[m]
        return carry

    lax.fori_loop(0, (ts // rb) * n_lane, conv_body, 0)

    lg = lg_ref[...]
    lb = lb_ref[...]
    og = og_ref[...]

    def norm_body(r, carry):
        sl = pl.ds(pl.multiple_of(r * norm_rows, norm_rows), norm_rows)
        a = cbuf[sl, :]
        xc = a - jnp.mean(a, axis=-1, keepdims=True)
        var = jnp.mean(xc * xc, axis=-1, keepdims=True)
        s = _silu(xc * lax.rsqrt(var + EPS) * lg + lb)
        y_ref[sl, :] = _rms_rows(s, og).astype(BF16)
        return carry

    lax.fori_loop(0, ts // norm_rows, norm_body, 0, unroll=norm_unroll)


def _branches(a_glu, z, conv_w, conv_b, ln_g, ln_b, out_g, sgu_w, bias_full, sgu_ln_g, sgu_ln_b,
              sgu_out_g, w_up, f_pad, seq, ts=256):
    t, c = a_glu.shape
    assert z.shape == (t, 2 * c)
    wu_rows = w_up.shape[0] // (t // ts)
    assert wu_rows * (t // ts) == w_up.shape[0] and wu_rows % HALO == 0
    taps = conv_w.shape[0]
    w_b = jnp.broadcast_to(conv_w[:, None, :], (taps, SUBLANE, c))
    cb_b = jnp.broadcast_to(conv_b, (SUBLANE, c))
    tiles_per_seq = seq // ts
    hb = ts // HALO
    n_hb = t // HALO
    kern = functools.partial(
        _branches_kernel,
        conv_kw=dict(ts=ts, tiles_per_seq=tiles_per_seq, taps=taps, rb=64, norm_rows=16,
                     norm_unroll=4),
        sgu_kw=dict(ts=ts, rows=16))
    row = lambda i: (0, 0)
    return pl.pallas_call(
        kern,
        grid=(t // ts,),
        in_specs=[
            pl.BlockSpec((HALO, c), lambda i: (jnp.maximum(i * hb - 1, 0), 0)),
            pl.BlockSpec((ts, c), lambda i: (i, 0)),
            pl.BlockSpec((HALO, c), lambda i: (jnp.minimum((i + 1) * hb, n_hb - 1), 0)),
            pl.BlockSpec(w_b.shape, lambda i: (0, 0, 0)),
            pl.BlockSpec((SUBLANE, c), row),
            pl.BlockSpec((1, c), row),
            pl.BlockSpec((1, c), row),
            pl.BlockSpec((1, c), row),
            pl.BlockSpec((wu_rows, w_up.shape[1]), lambda i: (i, 0)),
            pl.BlockSpec((ts, c), lambda i: (i, 0)),
            pl.BlockSpec((ts, c), lambda i: (i, 1)),
            pl.BlockSpec(sgu_w.shape, lambda i: (0, 0, 0)),
            pl.BlockSpec(bias_full.shape, row),
            pl.BlockSpec((1, c), row),
            pl.BlockSpec((1, c), row),
            pl.BlockSpec((1, c), row),
        ],
        out_specs=[pl.BlockSpec((ts, c), lambda i: (i, 0)),
                   pl.BlockSpec((wu_rows, 2 * f_pad), lambda i: (i, 0)),
                   pl.BlockSpec((ts, c), lambda i: (i, 0))],
        out_shape=[jax.ShapeDtypeStruct((t, c), BF16),
                   jax.ShapeDtypeStruct((w_up.shape[0], 2 * f_pad), BF16),
                   jax.ShapeDtypeStruct((t, c), BF16)],
        scratch_shapes=[pltpu.VMEM((ts + 2 * HALO, c), F32), pltpu.VMEM((ts, c), F32),
                        pltpu.VMEM((ts, c), BF16), pltpu.VMEM((ts, c), F32)],
        compiler_params=_params(("arbitrary",)),
        name="branches",
    )(a_glu, a_glu, a_glu, w_b, cb_b, ln_g, ln_b, out_g, w_up,
      z, z, sgu_w, bias_full, sgu_ln_g, sgu_ln_b, sgu_out_g)


def _sgubr_kernel(u_ref, v_ref, w_ref, bias_ref, lg_ref, lb_ref, og_ref, y_ref, vn_scr, b_scr,
                  *, ts, rows):
    lg = lg_ref[...]
    lb = lb_ref[...]
    og = og_ref[...]
    n_chunks = ts // CHUNK
    n_heads = v_ref.shape[1] // HEAD_DIM

    def ln_body(r, carry):
        sl = pl.ds(pl.multiple_of(r * rows, rows), rows)
        v = v_ref[sl, :]
        xc = v - jnp.mean(v, axis=-1, keepdims=True)
        var = jnp.mean(xc * xc, axis=-1, keepdims=True)
        vn_scr[sl, :] = (xc * lax.rsqrt(var + EPS) * lg + lb).astype(BF16)
        return carry

    lax.fori_loop(0, ts // rows, ln_body, 0, unroll=4)

    for h in range(n_heads):
        hs = slice(h * HEAD_DIM, (h + 1) * HEAD_DIM)
        rhs = jnp.concatenate(
            [vn_scr[ck * CHUNK:(ck + 1) * CHUNK, hs] for ck in range(n_chunks)], axis=1)
        mixed = _dot(w_ref[h], rhs)
        bias = bias_ref[:, hs]
        for ck in range(n_chunks):
            rs = slice(ck * CHUNK, (ck + 1) * CHUNK)
            m = mixed[:, ck * HEAD_DIM:(ck + 1) * HEAD_DIM] + bias
            b_scr[rs, hs] = u_ref[rs, hs] * m

    def rms_body(r, carry):
        sl = pl.ds(pl.multiple_of(r * rows, rows), rows)
        y_ref[sl, :] = _rms_rows(b_scr[sl, :], og).astype(BF16)
        return carry

    lax.fori_loop(0, ts // rows, rms_body, 0, unroll=4)


def _branches_kernel(prev_ref, main_ref, next_ref, cw_ref, cb_ref, clg_ref, clb_ref, cog_ref, wu_ref,
                     u_ref, v_ref, sw_ref, bias_ref, slg_ref, slb_ref, sog_ref,
                     ya_ref, wub_ref, yb_ref, buf, cbuf, vn_scr, b_scr, *, conv_kw, sgu_kw):
    _convbr_kernel(prev_ref, main_ref, next_ref, cw_ref, cb_ref, clg_ref, clb_ref, cog_ref, wu_ref,
                   ya_ref, wub_ref, buf, cbuf, **conv_kw)
    _sgubr_kernel(u_ref, v_ref, sw_ref, bias_ref, slg_ref, slb_ref, sog_ref, yb_ref, vn_scr, b_scr,
                  **sgu_kw)


def _outproj_kernel(ya_ref, yb_ref, wa_ref, wb_ref, x_ref, mod_ref, o_ref):
    y = _dot(ya_ref[...], wa_ref[...]) + _dot(yb_ref[...], wb_ref[...])
    o_ref[...] = x_ref[...] + mod_ref[0, 2:3, :] * y


def _outproj(ya, yb, w_out, x2d, mod, seq, tm=1024, tn=1024):
    t, c = ya.shape
    d = x2d.shape[1]
    tiles_per_seq = seq // tm
    return pl.pallas_call(
        _outproj_kernel,
        grid=(t // tm, d // tn),
        in_specs=[
            pl.BlockSpec((tm, c), lambda i, j: (i, 0)),
            pl.BlockSpec((tm, c), lambda i, j: (i, 0)),
            pl.BlockSpec((c, tn), lambda i, j: (0, j)),
            pl.BlockSpec((c, tn), lambda i, j: (1, j)),
            pl.BlockSpec((tm, tn), lambda i, j: (i, j)),
            pl.BlockSpec((1, N_MOD_PAD, tn), lambda i, j: (i // tiles_per_seq, 0, j)),
        ],
        out_specs=pl.BlockSpec((tm, tn), lambda i, j: (i, j)),
        out_shape=jax.ShapeDtypeStruct((t, d), F32),
        compiler_params=_params(("arbitrary", "arbitrary")),
        name="outproj",
    )(ya, yb, w_out, w_out, x2d, mod)


def _gated_act(g_ref, v_ref, cp_ref, act_ref, cols, tm):
    rb = 4 * SUBLANE
    n_grp = rb // SUBLANE
    row = lax.broadcasted_iota(jnp.int32, (SUBLANE, LANE), 0)
    for lc in range(cols // LANE):
        lane = slice(lc * LANE, (lc + 1) * LANE)
        w0, w1, w2, cb = (cp_ref[k, :, lane] for k in range(4))
        for r0 in range(0, tm, rb):
            g0 = HALO - SUBLANE + r0
            win = [g_ref[g0 + SUBLANE * m:g0 + SUBLANE * (m + 1), lane] for m in range(n_grp + 2)]
            dn = [pltpu.roll(w, 1, 0) for w in win]
            up = [pltpu.roll(w, SUBLANE - 1, 0) for w in win]
            out = []
            for m in range(n_grp):
                prev = jnp.where(row == 0, dn[m], dn[m + 1])
                nxt = jnp.where(row == SUBLANE - 1, up[m + 2], up[m + 1])
                conv = w0 * prev + w1 * win[m + 1] + w2 * nxt + cb
                out.append(_silu(conv) * v_ref[r0 + SUBLANE * m:r0 + SUBLANE * (m + 1), lane])
            act_ref[r0:r0 + rb, lane] = jnp.concatenate(out, axis=0).astype(BF16)


def _ffn_kernel(prev_ref, x1_hbm, next_ref, mod_ref, g_ref, wg_ref, wv_ref, cp_ref, wd_ref,
                gf_ref, o_ref, h_scr, g_scr, v_scr, act_scr, xbuf, xsem, *, tm,
                tiles_per_seq, n_f, last_cols, final_norm):
    i = pl.program_id(0)
    j = pl.program_id(1)
    rows = xbuf.shape[1]
    stream_x = functools.partial(_stream_rows, x1_hbm, i * tm, tm, xbuf, xsem)

    @pl.when(j == 0)
    def _():
        g = g_ref[...]
        shift = mod_ref[0, 3:4, :]
        scale = mod_ref[0, 4:5, :]
        first = (i % tiles_per_seq) == 0
        last = (i % tiles_per_seq) == tiles_per_seq - 1
        hp = _modulate(prev_ref[...], g, scale, shift)
        hn = _modulate(next_ref[...], g, scale, shift)
        h_scr[0:HALO, :] = jnp.where(first, 0.0, hp).astype(BF16)
        h_scr[HALO + tm:2 * HALO + tm, :] = jnp.where(last, 0.0, hn).astype(BF16)

        def modulate_chunk(r, x_ref):
            src = pl.ds(pl.multiple_of(r * rows, rows), rows)
            dst = pl.ds(pl.multiple_of(HALO + r * rows, HALO), rows)
            h_scr[dst, :] = _modulate(x_ref[...], g, scale, shift).astype(BF16)
            o_ref[src, :] = jnp.zeros((rows, o_ref.shape[1]), F32)

        stream_x(modulate_chunk)

    tf = v_scr.shape[1]

    def up_and_act(slot, cols):
        g_scr[:, :cols] = _dot(h_scr[...], wg_ref[:, :cols])
        v_scr[:, :cols] = _dot(h_scr[HALO:HALO + tm, :], wv_ref[:, :cols])
        _gated_act(g_scr, v_scr, cp_ref, act_scr.at[slot], cols, tm)

    def down_proj(slot, cols):
        o_ref[...] += _dot(act_scr.at[slot][:, :cols], wd_ref[:cols, :])

    last_slot = (n_f - 1) % 2
    assert n_f >= 3

    @pl.when(j == 0)
    def _():
        up_and_act(0, tf)

    @pl.when(jnp.logical_and(j > 0, j < n_f - 1))
    def _():
        slot = j % 2
        up_and_act(slot, tf)
        down_proj(1 - slot, tf)

    @pl.when(j == n_f - 1)
    def _():
        up_and_act(last_slot, last_cols)
        down_proj(1 - last_slot, tf)

    @pl.when(j == n_f)
    def _():
        down_proj(last_slot, last_cols)

    @pl.when(j == pl.num_programs(1) - 1)
    def _():
        gate = mod_ref[0, 5:6, :]
        gf = gf_ref[...]

        def residual_chunk(r, x_ref):
            sl = pl.ds(pl.multiple_of(r * rows, rows), rows)
            x2 = x_ref[...] + gate * o_ref[sl, :]
            o_ref[sl, :] = _rms_rows(x2, gf) if final_norm else x2

        stream_x(residual_chunk)


def _ffn(x1, mod, g_ffn, w_up, ffn_conv_w, ffn_conv_b, w_down, g_final, f, seq, final_norm,
         tm=1024, tf=FFN_TILE):
    t, d = x1.shape
    f_pad = w_down.shape[0]
    n_f = f_pad // tf
    tiles_per_seq = seq // tm
    hb = tm // HALO
    n_hb = t // HALO
    cp = jnp.concatenate([ffn_conv_w, ffn_conv_b], axis=0)
    cp = jnp.pad(cp, ((0, 0), (0, f_pad - cp.shape[1])))
    cp = jnp.broadcast_to(cp[:, None, :], (cp.shape[0], SUBLANE, f_pad))
    once = pl.Buffered(1)
    last_cols = f - (n_f - 1) * tf
    assert 0 < last_cols <= tf and last_cols % LANE == 0
    kern = functools.partial(_ffn_kernel, tm=tm, tiles_per_seq=tiles_per_seq, n_f=n_f,
                             last_cols=last_cols, final_norm=final_norm)
    up = lambda j: jnp.minimum(j, n_f - 1)
    down = lambda j: jnp.maximum(j - 1, 0)
    return pl.pallas_call(
        kern,
        grid=(t // tm, n_f + 1),
        in_specs=[
            pl.BlockSpec((HALO, d), lambda i, j: (jnp.maximum(i * hb - 1, 0), 0), pipeline_mode=once),
            pl.BlockSpec(memory_space=pl.ANY),
            pl.BlockSpec((HALO, d), lambda i, j: (jnp.minimum((i + 1) * hb, n_hb - 1), 0),
                         pipeline_mode=once),
            pl.BlockSpec((1, N_MOD_PAD, d), lambda i, j: (i // tiles_per_seq, 0, 0),
                         pipeline_mode=once),
            pl.BlockSpec((1, d), lambda i, j: (0, 0), pipeline_mode=once),
            pl.BlockSpec((d, tf), lambda i, j: (0, up(j))),
            pl.BlockSpec((d, tf), lambda i, j: (0, up(j) + n_f)),
            pl.BlockSpec((cp.shape[0], SUBLANE, tf), lambda i, j: (0, 0, up(j))),
            pl.BlockSpec((tf, d), lambda i, j: (down(j), 0)),
            pl.BlockSpec((1, d), lambda i, j: (0, 0), pipeline_mode=once),
        ],
        out_specs=pl.BlockSpec((tm, d), lambda i, j: (i, 0), pipeline_mode=once),
        out_shape=jax.ShapeDtypeStruct((t, d), F32),
        scratch_shapes=[pltpu.VMEM((tm + 2 * HALO, d), BF16),
                        pltpu.VMEM((tm + 2 * HALO, tf), F32),
                        pltpu.VMEM((tm, tf), F32),
                        pltpu.VMEM((2, tm, tf), BF16),
                        pltpu.VMEM((X_SLOTS, X_ROWS, d), F32),
                        pltpu.SemaphoreType.DMA((X_SLOTS,))],
        compiler_params=_params(("arbitrary", "arbitrary")),
        name="ffn",
    )(x1, x1, x1, mod, g_ffn, w_up, w_up, cp, w_down, g_final)


def kernel(x, c, w_ada, b_ada, g_mix, w_in, conv_w, conv_b, conv_ln_g, conv_ln_b, sgu_ln_g, sgu_ln_b, sgu_w, sgu_b, out_g_conv, out_g_sgu, w_out, g_ffn, w_up, ffn_conv_w, ffn_conv_b, w_down, g_final):
    bsz, seq, d = x.shape
    depth = w_ada.shape[0]
    d_conv = conv_w.shape[2]
    n_mod = w_ada.shape[2] // d
    xf = x.reshape(bsz * seq, d)
    c_pad = jnp.pad(c, ((0, (-bsz) % 8), (0, 0)))

    for l in range(depth):
        mod = _ada(c_pad, w_ada[l], b_ada[l][None, :])[:bsz].reshape(bsz, n_mod, d)
        mod = jnp.pad(mod, ((0, 0), (0, N_MOD_PAD - n_mod), (0, 0)))

        f_pad = -(-w_down.shape[1] // FFN_TILE) * FFN_TILE
        a_glu, z, wo_bf, wd_bf = _inproj(xf, mod, g_mix[l][None, :], w_in[l], w_out[l],
                                         w_down[l], f_pad, d_conv, seq)
        bias_full = jnp.repeat(sgu_b[l].T, HEAD_DIM, axis=1)
        ya, wu_bf, yb = _branches(
            a_glu, z, conv_w[l], conv_b[l][None, :], conv_ln_g[l][None, :], conv_ln_b[l][None, :],
            out_g_conv[l][None, :], sgu_w[l].astype(BF16), bias_full, sgu_ln_g[l][None, :],
            sgu_ln_b[l][None, :], out_g_sgu[l][None, :], w_up[l], f_pad, seq)
        x1 = _outproj(ya, yb, wo_bf, xf, mod, seq)
        xf = _ffn(x1, mod, g_ffn[l][None, :], wu_bf, ffn_conv_w[l],
                  ffn_conv_b[l][None, :], wd_bf, g_final[None, :], w_down.shape[1], seq,
                  final_norm=(l == depth - 1))

    return xf.reshape(bsz, seq, d)
```
